```python
import jax, jax.numpy as jnp
from jax import lax
import numpy as np

D_MODEL = 1024
BATCH = 16
SEQ = 2048
DEPTH = 1

HEAD_DIM = 64
H_FOX = 8
H_DSA = 8
W_FOX = H_FOX * HEAD_DIM
W_DSA = H_DSA * HEAD_DIM
MIX_WIDTH = W_FOX + W_DSA
H_IDX = 8
D_IDX = 64
TOPK_MAX = 256
Q_BLOCK = 128
EPS = 1e-6

IN_WIDTHS = (
    W_FOX, W_FOX, W_FOX,
    H_FOX,
    W_FOX,
    W_DSA, W_DSA, W_DSA,
    H_IDX * D_IDX,
    D_IDX,
    H_IDX,
    W_DSA,
)
N_IN = sum(IN_WIDTHS)
IN_SPLITS = tuple(int(v) for v in np.cumsum(IN_WIDTHS)[:-1])

kernel_name = "hybrid_fox_dsa_adaln_block"


def _rmsnorm(x, gain):
    x32 = x.astype(jnp.float32)
    y = x32 * lax.rsqrt(jnp.mean(x32 * x32, axis=-1, keepdims=True) + EPS)
    return y.astype(x.dtype) * gain


def _to_blocks(a):
    b, s = a.shape[:2]
    return jnp.swapaxes(a.reshape(b, s // Q_BLOCK, Q_BLOCK, *a.shape[2:]), 0, 1)


def _from_blocks(a):
    a = jnp.swapaxes(a, 0, 1)
    return a.reshape(a.shape[0], -1, *a.shape[3:])


def _alibi_slopes(n_heads):
    i = jnp.arange(1, n_heads + 1, dtype=jnp.float32)
    return jnp.exp2(-8.0 * i / n_heads)


def fox_attention(q, k, v, log_f):
    s_len = q.shape[1]
    scale = HEAD_DIM ** -0.5
    cum_f = jnp.cumsum(log_f.astype(jnp.float32), axis=1)
    cum_f_keys = jnp.swapaxes(cum_f, 1, 2)
    pos = jnp.arange(s_len)

    def block(args):
        qb, fq, tq = args
        logits = jnp.einsum('bqhd,bshd->bhqs', qb, k).astype(jnp.float32) * scale
        logits = logits + (jnp.swapaxes(fq, 1, 2)[..., :, None] - cum_f_keys[..., None, :])
        causal = pos[None, :] <= tq[:, None]
        logits = jnp.where(causal[None, None], logits, -jnp.inf)
        p = jax.nn.softmax(logits, axis=-1).astype(v.dtype)
        return jnp.einsum('bhqs,bshd->bqhd', p, v)

    out = lax.map(block, (_to_blocks(q), _to_blocks(cum_f), pos.reshape(-1, Q_BLOCK)))
    return _from_blocks(out)


def dsa_attention(q, k, v, q_idx, k_idx, w_idx):
    s_len = q.shape[1]
    top_k = min(TOPK_MAX, s_len // 4)
    scale = HEAD_DIM ** -0.5
    idx_scale = (D_IDX ** -0.5) * (H_IDX ** -0.5)
    slopes = _alibi_slopes(H_DSA)
    pos = jnp.arange(s_len)
    gather = jax.vmap(lambda arr, ii: arr[ii])

    def block(args):
        qb, qib, wb, tq = args
        s_h = jnp.einsum('bqhi,bsi->bqhs', qib, k_idx).astype(jnp.float32)
        index_score = jnp.einsum('bqh,bqhs->bqs', wb.astype(jnp.float32), jax.nn.relu(s_h)) * idx_scale
        causal = pos[None, :] <= tq[:, None]
        index_score = jnp.where(causal[None], index_score, -jnp.inf)
        _, sel = lax.top_k(index_score, top_k)
        k_sel = gather(k, sel)
        v_sel = gather(v, sel)
        logits = jnp.einsum('bqhd,bqkhd->bhqk', qb, k_sel).astype(jnp.float32) * scale
        dist = (tq[None, :, None] - sel).astype(jnp.float32)
        logits = logits - slopes[None, :, None, None] * dist[:, None]
        valid = sel <= tq[None, :, None]
        logits = jnp.where(valid[:, None], logits, -jnp.inf)
        p = jax.nn.softmax(logits, axis=-1).astype(v.dtype)
        return jnp.einsum('bhqk,bqkhd->bqhd', p, v_sel)

    out = lax.map(block, (_to_blocks(q), _to_blocks(q_idx), _to_blocks(w_idx),
                          pos.reshape(-1, Q_BLOCK)))
    return _from_blocks(out)


def setup_inputs(seed: int = 0) -> dict:
    key = jax.random.key(seed)
    ks = jax.random.split(key, 10)
    f32 = jnp.float32
    x = jax.random.normal(ks[0], (BATCH, SEQ, D_MODEL), f32)
    c = jax.random.normal(ks[1], (BATCH, D_MODEL), f32)
    w_mod = jax.random.normal(ks[2], (DEPTH, D_MODEL, 3 * D_MODEL), f32) * (0.5 * D_MODEL ** -0.5)
    b_mod = jax.random.normal(ks[3], (DEPTH, 3 * D_MODEL), f32) * 0.02
    norm_gain = 1.0 + 0.05 * jax.random.normal(ks[4], (DEPTH, D_MODEL), f32)
    w_in = jax.random.normal(ks[5], (DEPTH, D_MODEL, N_IN), f32) * (D_MODEL ** -0.5)
    b_forget = jax.random.normal(ks[6], (DEPTH, H_FOX), f32) * 0.1
    w_out = jax.random.normal(ks[7], (DEPTH, MIX_WIDTH, D_MODEL), f32) * (MIX_WIDTH ** -0.5)
    final_gain = 1.0 + 0.05 * jax.random.normal(ks[8], (D_MODEL,), f32)
    return {"x": x, "c": c, "w_mod": w_mod, "b_mod": b_mod, "norm_gain": norm_gain,
            "w_in": w_in, "b_forget": b_forget, "w_out": w_out, "final_gain": final_gain}


def reference(x, c, w_mod, b_mod, norm_gain, w_in, b_forget, w_out, final_gain):
    b, s, _ = x.shape
    c_act = jax.nn.silu(c)
    for layer in range(DEPTH):
        mod = c_act @ w_mod[layer] + b_mod[layer]
        shift, scale, gate = jnp.split(mod, 3, axis=-1)
        h = _rmsnorm(x, norm_gain[layer]) * (1.0 + scale[:, None, :]) + shift[:, None, :]

        proj = h @ w_in[layer]
        (q_a, k_a, v_a, f_logit, g_a,
         q_b, k_b, v_b, q_i, k_i, w_i, g_b) = jnp.split(proj, IN_SPLITS, axis=-1)

        log_f = jax.nn.log_sigmoid((f_logit + b_forget[layer]).astype(jnp.float32))
        y_a = fox_attention(q_a.reshape(b, s, H_FOX, HEAD_DIM),
                            k_a.reshape(b, s, H_FOX, HEAD_DIM),
                            v_a.reshape(b, s, H_FOX, HEAD_DIM), log_f).reshape(b, s, W_FOX)
        y_a = y_a * jax.nn.silu(g_a)

        y_b = dsa_attention(q_b.reshape(b, s, H_DSA, HEAD_DIM),
                            k_b.reshape(b, s, H_DSA, HEAD_DIM),
                            v_b.reshape(b, s, H_DSA, HEAD_DIM),
                            q_i.reshape(b, s, H_IDX, D_IDX), k_i, w_i).reshape(b, s, W_DSA)
        y_b = y_b * jax.nn.silu(g_b)

        y = jnp.concatenate([y_a, y_b], axis=-1) @ w_out[layer]
        x = x + gate[:, None, :] * y
    return _rmsnorm(x, final_gain)
```

```python
import functools

import jax
import jax.numpy as jnp
import numpy as np
from jax import lax
from jax.experimental import pallas as pl
from jax.experimental.pallas import tpu as pltpu

F32 = jnp.float32
BF16 = jnp.bfloat16
I32 = jnp.int32

D_MODEL = 1024
HEAD_DIM = 64
N_HEADS = 8
GROUP_W = N_HEADS * HEAD_DIM
TOPK_MAX = 256
EPS = 1e-6
LOG2E = 1.4426950408889634
NEG = -1e30
INT_MIN = -(2 ** 31)

LANES = 128
TILE = 256
VMEM_LIMIT = 56 * 1024 * 1024

G_QA, G_VA, G_GA, G_QB, G_VB, G_GB, G_QI = range(7)
N_TGROUPS = 7
WI_ROWS = 16
NAT_KA = 0
NAT_KB = N_HEADS
NAT_F = 2 * N_HEADS
NAT_KI = 2 * N_HEADS + 1
N_NAT = (2 * N_HEADS + 2) * LANES


def _split3(v):
    hi = v.astype(BF16)
    r1 = v - hi.astype(F32)
    mid = r1.astype(BF16)
    r2 = r1 - mid.astype(F32)
    return hi, mid, r2.astype(BF16)


def _dot(a, b):
    return jnp.dot(a, b, preferred_element_type=F32)


def _dot_nt(a, b):
    return lax.dot_general(a, b, (((1,), (1,)), ((), ())), preferred_element_type=F32)


def _dot_tn(a, b):
    return lax.dot_general(a, b, (((0,), (0,)), ((), ())), preferred_element_type=F32)


def _mod_kernel(c_ref, w_ref, b_ref, o_ref):
    c = c_ref[...]
    act = c / (1.0 + jnp.exp(-c))
    a_hi = act.astype(BF16)
    a_lo = (act - a_hi.astype(F32)).astype(BF16)
    w = w_ref[...]
    w_hi = w.astype(BF16)
    w_lo = (w - w_hi.astype(F32)).astype(BF16)
    o_ref[...] = _dot(a_hi, w_hi) + _dot(a_hi, w_lo) + _dot(a_lo, w_hi) + b_ref[...]


def _mod_call(c, w_mod, b_mod):
    b = c.shape[0]
    n = w_mod.shape[1]
    bn = D_MODEL
    return pl.pallas_call(
        _mod_kernel,
        grid=(n // bn,),
        in_specs=[
            pl.BlockSpec((b, D_MODEL), lambda j: (0, 0)),
            pl.BlockSpec((D_MODEL, bn), lambda j: (0, j)),
            pl.BlockSpec((1, bn), lambda j: (0, j)),
        ],
        out_specs=pl.BlockSpec((b, bn), lambda j: (0, j)),
        out_shape=jax.ShapeDtypeStruct((b, n), F32),
        compiler_params=pltpu.CompilerParams(dimension_semantics=("arbitrary",)),
        name="mod",
    )(c, w_mod, b_mod.reshape(1, n))


def _proj_kernel(x_ref, mod_ref, gain_ref, wnat_ref, wt_ref, bf_ref, pos_ref, e_ref,
                 ka_ref, kb_ref, ki_ref, t_ref, wi_ref, carry_ref, *, tm, q_scale):
    si = pl.program_id(1)

    @pl.when(si == 0)
    def _():
        carry_ref[...] = jnp.zeros_like(carry_ref)

    x = x_ref[0]
    ms = jnp.mean(x * x, axis=-1, keepdims=True)
    xn = (x * lax.rsqrt(ms + EPS)) * gain_ref[...]
    shift = mod_ref[0, 0:1, :]
    scale = mod_ref[0, 1:2, :]
    h = (xn * (1.0 + scale) + shift).astype(BF16)

    pn = _dot(h, wnat_ref[...])
    pt = _dot_nt(wt_ref[...], h)

    lane = lax.broadcasted_iota(I32, (tm, LANES), 1)
    fl = pn[:, NAT_F * LANES:(NAT_F + 1) * LANES] + bf_ref[...]
    logf = jnp.minimum(fl, 0.0) - jnp.log(1.0 + jnp.exp(-jnp.abs(fl)))
    logf = jnp.where(lane < N_HEADS, logf, 0.0)
    row = lax.broadcasted_iota(I32, (tm, tm), 0)
    col = lax.broadcasted_iota(I32, (tm, tm), 1)
    tri = jnp.where(col <= row, 1.0, 0.0).astype(BF16)
    l_hi, l_mid, l_lo = _split3(logf)
    cum = carry_ref[...] + (_dot(tri, l_hi) + _dot(tri, l_mid) + _dot(tri, l_lo))
    carry_ref[...] = cum[tm - 1:tm, :]

    f_hi, f_mid, f_lo = _split3(cum * (-LOG2E))
    faug = _dot(f_hi, e_ref[0]) + _dot(f_mid, e_ref[1]) + _dot(f_lo, e_ref[2])
    pos = pos_ref[...].astype(F32)
    for hh in range(N_HEADS):
        a0 = (NAT_KA + hh) * LANES
        ka_ref[0, hh] = (pn[:, a0:a0 + LANES] + faug[:, hh * LANES:(hh + 1) * LANES]).astype(BF16)
        b0 = (NAT_KB + hh) * LANES
        kb_ref[0, hh] = (pn[:, b0:b0 + LANES] + pos).astype(BF16)
    ki_ref[0] = pn[:, NAT_KI * LANES:(NAT_KI + 1) * LANES].astype(BF16)

    def put(group, val):
        for cc in range(tm // TILE):
            t_ref[0, cc, group * GROUP_W:(group + 1) * GROUP_W, :] = (
                val[:, cc * TILE:(cc + 1) * TILE].astype(BF16))

    def rows(group):
        return pt[group * GROUP_W:(group + 1) * GROUP_W, :]

    def silu(v):
        return v / (1.0 + jnp.exp(-v))

    put(G_QA, rows(G_QA) * q_scale)
    put(G_VA, rows(G_VA))
    put(G_GA, silu(rows(G_GA)))
    put(G_QB, rows(G_QB) * q_scale)
    put(G_VB, rows(G_VB))
    put(G_GB, silu(rows(G_GB)))
    put(G_QI, rows(G_QI))
    wi_ref[0] = pt[N_TGROUPS * GROUP_W:N_TGROUPS * GROUP_W + N_HEADS, :]


def _proj_call(x, mod3, gain, w_nat, w_t, bf_row, pos_aug, e_mats, *, tm):
    b, s, _ = x.shape
    ns = s // tm
    nt = s // TILE
    t_rows = N_TGROUPS * GROUP_W
    kern = functools.partial(
        _proj_kernel, tm=tm,
        q_scale=float(HEAD_DIM ** -0.5 * LOG2E))
    const = dict(pipeline_mode=pl.Buffered(1))
    return pl.pallas_call(
        kern,
        grid=(b, ns),
        in_specs=[
            pl.BlockSpec((1, tm, D_MODEL), lambda i, j: (i, j, 0)),
            pl.BlockSpec((1, 3, D_MODEL), lambda i, j: (i, 0, 0)),
            pl.BlockSpec((1, D_MODEL), lambda i, j: (0, 0)),
            pl.BlockSpec((D_MODEL, N_NAT), lambda i, j: (0, 0), **const),
            pl.BlockSpec((t_rows + WI_ROWS, D_MODEL), lambda i, j: (0, 0), **const),
            pl.BlockSpec((1, LANES), lambda i, j: (0, 0)),
            pl.BlockSpec((tm, LANES), lambda i, j: (j, 0)),
            pl.BlockSpec((3, LANES, N_HEADS * LANES), lambda i, j: (0, 0, 0), **const),
        ],
        out_specs=[
            pl.BlockSpec((1, N_HEADS, tm, LANES), lambda i, j: (i, 0, j, 0)),
            pl.BlockSpec((1, N_HEADS, tm, LANES), lambda i, j: (i, 0, j, 0)),
            pl.BlockSpec((1, tm, LANES), lambda i, j: (i, j, 0)),
            pl.BlockSpec((1, tm // TILE, t_rows, TILE), lambda i, j: (i, j, 0, 0)),
            pl.BlockSpec((1, N_HEADS, tm), lambda i, j: (i, 0, j)),
        ],
        out_shape=[
            jax.ShapeDtypeStruct((b, N_HEADS, s, LANES), BF16),
            jax.ShapeDtypeStruct((b, N_HEADS, s, LANES), BF16),
            jax.ShapeDtypeStruct((b, s, LANES), BF16),
            jax.ShapeDtypeStruct((b, nt, t_rows, TILE), BF16),
            jax.ShapeDtypeStruct((b, N_HEADS, s), F32),
        ],
        scratch_shapes=[pltpu.VMEM((1, LANES), F32)],
        compiler_params=pltpu.CompilerParams(
            dimension_semantics=("arbitrary", "arbitrary"), vmem_limit_bytes=VMEM_LIMIT),
        name="proj",
    )(x, mod3, gain, w_nat, w_t, bf_row, pos_aug, e_mats)


def _softmax_tile(s, v_t, m, l, acc):
    m_new = jnp.maximum(m, jnp.max(s, axis=0, keepdims=True))
    p = jnp.exp2(s - m_new)
    alpha = jnp.exp2(m - m_new)
    l_new = alpha * l + jnp.sum(p, axis=0, keepdims=True)
    acc_new = alpha * acc + _dot(v_t, p.astype(BF16))
    return m_new, l_new, acc_new


def _init_state(tq):
    return (jnp.full((1, tq), NEG, F32), jnp.zeros((1, tq), F32), jnp.zeros((HEAD_DIM, tq), F32))


def _fox_kernel(q_ref, k_ref, v_ref, g_ref, qaug_ref, o_ref, *, hb):
    qi = pl.program_id(2)
    t = TILE
    qz = [jnp.concatenate([q_ref[0, 0, j * HEAD_DIM:(j + 1) * HEAD_DIM, :], qaug_ref[...]], axis=0)
          for j in range(hb)]

    def scores(j, ki):
        k_t = k_ref[0, j, pl.ds(pl.multiple_of(ki * t, t), t), :]
        return _dot(k_t, qz[j])

    def v_tile(j, ki):
        return v_ref[0, ki, j * HEAD_DIM:(j + 1) * HEAD_DIM, :]

    def body(ki, carry):
        return tuple(_softmax_tile(scores(j, ki), v_tile(j, ki), *carry[j]) for j in range(hb))

    state = lax.fori_loop(0, qi, body, tuple(_init_state(t) for _ in range(hb)))

    srow = lax.broadcasted_iota(I32, (t, t), 0)
    tcol = lax.broadcasted_iota(I32, (t, t), 1)
    causal = srow <= tcol
    for j in range(hb):
        s = jnp.where(causal, scores(j, qi), NEG)
        _, l, acc = _softmax_tile(s, v_tile(j, qi), *state[j])
        gate = g_ref[0, 0, j * HEAD_DIM:(j + 1) * HEAD_DIM, :].astype(F32)
        o_ref[0, 0, j * HEAD_DIM:(j + 1) * HEAD_DIM, :] = ((acc / l) * gate).astype(BF16)


def _fox_call(ka, t_arr, qaug, *, hb):
    b, _, s, _ = ka.shape
    nt = s // TILE
    gpb = GROUP_W // (HEAD_DIM * hb)
    rb = HEAD_DIM * hb
    return pl.pallas_call(
        functools.partial(_fox_kernel, hb=hb),
        grid=(b, N_HEADS // hb, nt),
        in_specs=[
            pl.BlockSpec((1, 1, rb, TILE), lambda i, h, q: (i, q, G_QA * gpb + h, 0)),
            pl.BlockSpec((1, hb, s, LANES), lambda i, h, q: (i, h, 0, 0)),
            pl.BlockSpec((1, nt, rb, TILE), lambda i, h, q: (i, 0, G_VA * gpb + h, 0)),
            pl.BlockSpec((1, 1, rb, TILE), lambda i, h, q: (i, q, G_GA * gpb + h, 0)),
            pl.BlockSpec((HEAD_DIM, TILE), lambda i, h, q: (0, 0)),
        ],
        out_specs=pl.BlockSpec((1, 1, rb, TILE), lambda i, h, q: (i, q, h, 0)),
        out_shape=jax.ShapeDtypeStruct((b, nt, GROUP_W, TILE), BF16),
        compiler_params=pltpu.CompilerParams(
            dimension_semantics=("arbitrary", "arbitrary", "arbitrary"),
            vmem_limit_bytes=VMEM_LIMIT),
        name="fox",
    )(t_arr, ka, t_arr, t_arr, qaug)


def _dsa_kernel(qi_ref, wi_ref, ki_ref, q_ref, k_ref, v_ref, g_ref, qaug_ref, o_ref,
                key_ref, bias_ref, thr_ref, lim_ref, *, top_k, idx_bits, idx_scale, hb):
    qidx = pl.program_id(1)
    t = TILE
    nkv = qidx + 1
    srow = lax.broadcasted_iota(I32, (t, t), 0)
    tcol = lax.broadcasted_iota(I32, (t, t), 1)
    zpad = jnp.zeros((LANES - HEAD_DIM, t), BF16)

    qiz = [jnp.concatenate([qi_ref[0, 0, j * HEAD_DIM:(j + 1) * HEAD_DIM, :], zpad], axis=0)
           for j in range(N_HEADS)]
    wrow = [wi_ref[0, j:j + 1, :] for j in range(N_HEADS)]

    def score_keys(ki):
        k_t = ki_ref[0, pl.ds(pl.multiple_of(ki * t, t), t), :]
        sc = jnp.zeros((t, t), F32)
        for j in range(N_HEADS):
            sc = sc + wrow[j] * jnp.maximum(_dot(k_t, qiz[j]), 0.0)
        bits = lax.bitcast_convert_type(sc * idx_scale, I32)
        return bits ^ ((bits >> 31) & 0x7FFFFFFF)

    def p1(ki, c):
        key_ref[ki] = score_keys(ki)
        return c

    lax.fori_loop(0, qidx, p1, 0)
    key_ref[qidx] = jnp.where(srow <= tcol, score_keys(qidx), INT_MIN)

    def count(pred):
        def body(ki, c):
            sidx = srow + ki * t
            ind = jnp.where(pred(key_ref[ki], sidx), 1, 0).astype(I32)
            return c + jnp.sum(ind.reshape(t // 8, 8, t), axis=0)
        c8 = lax.fori_loop(0, nkv, body, jnp.zeros((8, t), I32))
        return jnp.sum(c8, axis=0, keepdims=True)

    thr_ref[...] = jnp.full((1, t), INT_MIN, I32)
    lim_ref[...] = jnp.zeros((1, t), I32)

    @pl.when(nkv * t > top_k)
    def _():
        zero = jnp.zeros((1, t), I32)
        c0 = count(lambda kk, sidx: kk >= zero)
        thr0 = jnp.where(c0 >= top_k, 0, INT_MIN).astype(I32)

        def bit_body(i, thr):
            cand = thr + lax.shift_left(jnp.int32(1), 30 - i)
            c = count(lambda kk, sidx: kk >= cand)
            return jnp.where(c >= top_k, cand, thr)

        thr = lax.fori_loop(0, 31, bit_body, thr0)
        thr_ref[...] = thr

        active = thr > INT_MIN
        c_ge = count(lambda kk, sidx: kk >= thr)
        c_gt = count(lambda kk, sidx: kk > thr)
        need = top_k - c_gt
        big = jnp.int32(2 ** idx_bits - 1)
        lim_ref[...] = jnp.where(active, big, 0)
        excess = jnp.max(jnp.where(active & (c_ge > top_k), 1, 0))

        @pl.when(excess > 0)
        def _():
            def lim_body(i, lim):
                cand = lim + lax.shift_left(jnp.int32(1), idx_bits - 1 - i)
                c = count(lambda kk, sidx: (kk == thr) & (sidx < cand))
                return jnp.where(c <= need, cand, lim)

            lim = lax.fori_loop(0, idx_bits, lim_body, jnp.zeros((1, t), I32))
            lim_ref[...] = jnp.where(active, lim, 0)

    thr = thr_ref[...]
    lim = lim_ref[...]

    def p3(ki, c):
        kk = key_ref[ki]
        sidx = srow + ki * t
        sel = (kk > thr) | ((kk == thr) & (sidx < lim))
        bias_ref[ki] = jnp.where(sel, 0.0, NEG).astype(F32)
        return c

    lax.fori_loop(0, nkv, p3, 0)

    for j0 in range(0, N_HEADS, hb):
        heads = list(range(j0, j0 + hb))
        qz = {j: jnp.concatenate([q_ref[0, 0, j * HEAD_DIM:(j + 1) * HEAD_DIM, :], qaug_ref[j]],
                                 axis=0) for j in heads}

        def body(ki, carry, heads=heads, qz=qz):
            bias = bias_ref[ki]
            out = []
            for n, j in enumerate(heads):
                k_t = k_ref[0, j, pl.ds(pl.multiple_of(ki * t, t), t), :]
                s = _dot(k_t, qz[j]) + bias
                v_t = v_ref[0, ki, j * HEAD_DIM:(j + 1) * HEAD_DIM, :]
                out.append(_softmax_tile(s, v_t, *carry[n]))
            return tuple(out)

        state = lax.fori_loop(0, nkv, body, tuple(_init_state(t) for _ in heads))
        for n, j in enumerate(heads):
            _, l, acc = state[n]
            gate = g_ref[0, 0, j * HEAD_DIM:(j + 1) * HEAD_DIM, :].astype(F32)
            o_ref[0, 0, j * HEAD_DIM:(j + 1) * HEAD_DIM, :] = ((acc / l) * gate).astype(BF16)


def _dsa_call(t_arr, wi, ki, kb, qaug, *, hb):
    b, _, s, _ = kb.shape
    nt = s // TILE
    top_k = min(TOPK_MAX, s // 4)
    idx_bits = int(np.ceil(np.log2(s))) + 1
    kern = functools.partial(_dsa_kernel, top_k=top_k, idx_bits=idx_bits, hb=hb,
                             idx_scale=float(HEAD_DIM ** -0.5 * N_HEADS ** -0.5))
    return pl.pallas_call(
        kern,
        grid=(b, nt),
        in_specs=[
            pl.BlockSpec((1, 1, GROUP_W, TILE), lambda i, q: (i, q, G_QI, 0)),
            pl.BlockSpec((1, N_HEADS, TILE), lambda i, q: (i, 0, q)),
            pl.BlockSpec((1, s, LANES), lambda i, q: (i, 0, 0)),
            pl.BlockSpec((1, 1, GROUP_W, TILE), lambda i, q: (i, q, G_QB, 0)),
            pl.BlockSpec((1, N_HEADS, s, LANES), lambda i, q: (i, 0, 0, 0)),
            pl.BlockSpec((1, nt, GROUP_W, TILE), lambda i, q: (i, 0, G_VB, 0)),
            pl.BlockSpec((1, 1, GROUP_W, TILE), lambda i, q: (i, q, G_GB, 0)),
            pl.BlockSpec((N_HEADS, HEAD_DIM, TILE), lambda i, q: (0, 0, 0)),
        ],
        out_specs=pl.BlockSpec((1, 1, GROUP_W, TILE), lambda i, q: (i, q, 0, 0)),
        out_shape=jax.ShapeDtypeStruct((b, nt, GROUP_W, TILE), BF16),
        scratch_shapes=[
            pltpu.VMEM((nt, TILE, TILE), I32),
            pltpu.VMEM((nt, TILE, TILE), F32),
            pltpu.VMEM((1, TILE), I32),
            pltpu.VMEM((1, TILE), I32),
        ],
        compiler_params=pltpu.CompilerParams(
            dimension_semantics=("arbitrary", "arbitrary"), vmem_limit_bytes=VMEM_LIMIT),
        name="dsa",
    )(t_arr, wi, ki, t_arr, kb, t_arr, t_arr, qaug)


def _out_kernel(x_ref, ya_ref, yb_ref, w_ref, gate_ref, gain_ref, o_ref):
    y_t = jnp.concatenate([ya_ref[0, 0], yb_ref[0, 0]], axis=0)
    y = _dot_tn(y_t, w_ref[...])
    z = x_ref[0] + gate_ref[0] * y
    ms = jnp.mean(z * z, axis=-1, keepdims=True)
    o_ref[0] = (z * lax.rsqrt(ms + EPS)) * gain_ref[...]


def _out_call(x, ya, yb, w_out, gate3, final_gain):
    b, s, _ = x.shape
    nt = s // TILE
    return pl.pallas_call(
        _out_kernel,
        grid=(b, nt),
        in_specs=[
            pl.BlockSpec((1, TILE, D_MODEL), lambda i, q: (i, q, 0)),
            pl.BlockSpec((1, 1, GROUP_W, TILE), lambda i, q: (i, q, 0, 0)),
            pl.BlockSpec((1, 1, GROUP_W, TILE), lambda i, q: (i, q, 0, 0)),
            pl.BlockSpec((2 * GROUP_W, D_MODEL), lambda i, q: (0, 0)),
            pl.BlockSpec((1, 1, D_MODEL), lambda i, q: (i, 0, 0)),
            pl.BlockSpec((1, D_MODEL), lambda i, q: (0, 0)),
        ],
        out_specs=pl.BlockSpec((1, TILE, D_MODEL), lambda i, q: (i, q, 0)),
        out_shape=jax.ShapeDtypeStruct((b, s, D_MODEL), F32),
        compiler_params=pltpu.CompilerParams(
            dimension_semantics=("arbitrary", "arbitrary"), vmem_limit_bytes=VMEM_LIMIT),
        name="outproj",
    )(x, ya, yb, w_out, gate3, final_gain)


def _layout_weights(w_in):
    widths = (GROUP_W, GROUP_W, GROUP_W, N_HEADS, GROUP_W, GROUP_W, GROUP_W, GROUP_W,
              N_HEADS * HEAD_DIM, HEAD_DIM, N_HEADS, GROUP_W)
    offs = np.concatenate([[0], np.cumsum(widths)])
    (w_qa, w_ka, w_va, w_f, w_ga, w_qb, w_kb, w_vb, w_qi, w_ki, w_wi, w_gb) = (
        w_in[:, offs[n]:offs[n + 1]] for n in range(len(widths)))

    def pad_heads(w):
        w = w.reshape(D_MODEL, N_HEADS, HEAD_DIM)
        return jnp.pad(w, ((0, 0), (0, 0), (0, LANES - HEAD_DIM))).reshape(D_MODEL, N_HEADS * LANES)

    def pad_cols(w):
        return jnp.pad(w, ((0, 0), (0, LANES - w.shape[1])))

    w_nat = jnp.concatenate([pad_heads(w_ka), pad_heads(w_kb), pad_cols(w_f), pad_cols(w_ki)], axis=1)
    w_t = jnp.concatenate([w_qa, w_va, w_ga, w_qb, w_vb, w_gb, w_qi,
                           jnp.pad(w_wi, ((0, 0), (0, WI_ROWS - N_HEADS)))], axis=1).T
    return w_nat.astype(BF16), w_t.astype(BF16)


def _constants(s):
    e = np.zeros((3, LANES, N_HEADS * LANES), np.float32)
    for i in range(3):
        for h in range(N_HEADS):
            e[i, h, h * LANES + HEAD_DIM + i] = 1.0
    pos = np.zeros((s, LANES), np.float32)
    sidx = np.arange(s)
    for i in range(3):
        pos[:, HEAD_DIM + i] = sidx // 64
        pos[:, HEAD_DIM + 3 + i] = sidx % 64
    qa = np.zeros((HEAD_DIM, TILE), np.float32)
    qa[0:3, :] = 1.0
    return jnp.asarray(e, BF16), jnp.asarray(pos, BF16), jnp.asarray(qa, BF16)


def _alibi_rows():
    i = jnp.arange(1, N_HEADS + 1, dtype=F32)
    slopes = jnp.exp2(-8.0 * i / N_HEADS) * LOG2E
    c64 = _split3(slopes * 64.0)
    c1 = _split3(slopes)
    rows = jnp.stack(list(c64) + list(c1), axis=1)
    rows = jnp.pad(rows, ((0, 0), (0, HEAD_DIM - 6)))
    return jnp.broadcast_to(rows[:, :, None], (N_HEADS, HEAD_DIM, TILE)).astype(BF16)


def kernel(x, c, w_mod, b_mod, norm_gain, w_in, b_forget, w_out, final_gain):
    b, s, d = x.shape
    assert d == D_MODEL and s % TILE == 0
    assert w_mod.shape[0] == 1, "single-layer block"
    tm = TILE

    mod = _mod_call(c, w_mod[0], b_mod[0])
    mod3 = mod.reshape(b, 3, D_MODEL)
    w_nat, w_t = _layout_weights(w_in[0])
    e_mats, pos_aug, qaug_a = _constants(s)
    bf_row = jnp.pad(b_forget[0], (0, LANES - N_HEADS)).reshape(1, LANES)

    ka, kb, ki, t_arr, wi = _proj_call(
        x, mod3, norm_gain[0].reshape(1, D_MODEL), w_nat, w_t, bf_row, pos_aug, e_mats, tm=tm)

    ya = _fox_call(ka, t_arr, qaug_a, hb=2)
    yb = _dsa_call(t_arr, wi, ki, kb, _alibi_rows(), hb=2)

    gate3 = mod3[:, 2:3, :]
    return _out_call(x, ya, yb, w_out[0].astype(BF16), gate3, final_gain.reshape(1, D_MODEL))
```

```python
import functools

import jax
import jax.numpy as jnp
import numpy as np
from jax import lax
from jax.experimental import pallas as pl
from jax.experimental.pallas import tpu as pltpu

F32 = jnp.float32
BF16 = jnp.bfloat16
I32 = jnp.int32

D_MODEL = 1024
HEAD_DIM = 64
N_HEADS = 8
GROUP_W = N_HEADS * HEAD_DIM
TOPK_MAX = 256
EPS = 1e-6
LOG2E = 1.4426950408889634
NEG = -1e30
INT_MIN = -(2 ** 31)

LANES = 128
TILE = 256
HEADS_PER_BODY = 4
VMEM_LIMIT = 56 * 1024 * 1024

G_QA, G_VA, G_GA, G_QB, G_VB, G_GB, G_QI = range(7)
N_TGROUPS = 7
WI_ROWS = 16
NAT_KA = 0
NAT_KB = N_HEADS
NAT_F = 2 * N_HEADS
NAT_KI = 2 * N_HEADS + 1
N_NAT = (2 * N_HEADS + 2) * LANES


def _split3(v):
    hi = v.astype(BF16)
    r1 = v - hi.astype(F32)
    mid = r1.astype(BF16)
    r2 = r1 - mid.astype(F32)
    return hi, mid, r2.astype(BF16)


def _dot(a, b):
    return jnp.dot(a, b, preferred_element_type=F32)


def _dot_nt(a, b):
    return lax.dot_general(a, b, (((1,), (1,)), ((), ())), preferred_element_type=F32)


def _dot_tn(a, b):
    return lax.dot_general(a, b, (((0,), (0,)), ((), ())), preferred_element_type=F32)


def _mod_kernel(c_ref, w_ref, b_ref, o_ref):
    c = c_ref[...]
    act = c / (1.0 + jnp.exp(-c))
    a_hi = act.astype(BF16)
    a_lo = (act - a_hi.astype(F32)).astype(BF16)
    w = w_ref[...]
    w_hi = w.astype(BF16)
    w_lo = (w - w_hi.astype(F32)).astype(BF16)
    o_ref[...] = _dot(a_hi, w_hi) + _dot(a_hi, w_lo) + _dot(a_lo, w_hi) + b_ref[...]


def _mod_call(c, w_mod, b_mod):
    b = c.shape[0]
    n = w_mod.shape[1]
    bn = D_MODEL
    return pl.pallas_call(
        _mod_kernel,
        grid=(n // bn,),
        in_specs=[
            pl.BlockSpec((b, D_MODEL), lambda j: (0, 0)),
            pl.BlockSpec((D_MODEL, bn), lambda j: (0, j)),
            pl.BlockSpec((1, bn), lambda j: (0, j)),
        ],
        out_specs=pl.BlockSpec((b, bn), lambda j: (0, j)),
        out_shape=jax.ShapeDtypeStruct((b, n), F32),
        compiler_params=pltpu.CompilerParams(dimension_semantics=("arbitrary",)),
        name="mod",
    )(c, w_mod, b_mod.reshape(1, n))


def _proj_kernel(x_ref, mod_ref, gain_ref, wnat_ref, wt_ref, bf_ref, pos_ref, e_ref,
                 ka_ref, kb_ref, ki_ref, t_ref, wi_ref, carry_ref, *, tm, q_scale):
    si = pl.program_id(1)

    @pl.when(si == 0)
    def _():
        carry_ref[...] = jnp.zeros_like(carry_ref)

    x = x_ref[0]
    ms = jnp.mean(x * x, axis=-1, keepdims=True)
    xn = (x * lax.rsqrt(ms + EPS)) * gain_ref[...]
    shift = mod_ref[0, 0:1, :]
    scale = mod_ref[0, 1:2, :]
    h = (xn * (1.0 + scale) + shift).astype(BF16)

    pn = _dot(h, wnat_ref[...])
    pt = _dot_nt(wt_ref[...], h)

    lane = lax.broadcasted_iota(I32, (tm, LANES), 1)
    fl = pn[:, NAT_F * LANES:(NAT_F + 1) * LANES] + bf_ref[...]
    logf = jnp.minimum(fl, 0.0) - jnp.log(1.0 + jnp.exp(-jnp.abs(fl)))
    logf = jnp.where(lane < N_HEADS, logf, 0.0)
    row = lax.broadcasted_iota(I32, (tm, tm), 0)
    col = lax.broadcasted_iota(I32, (tm, tm), 1)
    tri = jnp.where(col <= row, 1.0, 0.0).astype(BF16)
    l_hi, l_mid, l_lo = _split3(logf)
    cum = carry_ref[...] + (_dot(tri, l_hi) + _dot(tri, l_mid) + _dot(tri, l_lo))
    carry_ref[...] = cum[tm - 1:tm, :]

    f_hi, f_mid, f_lo = _split3(cum * (-LOG2E))
    faug = _dot(f_hi, e_ref[0]) + _dot(f_mid, e_ref[1]) + _dot(f_lo, e_ref[2])
    pos = pos_ref[...].astype(F32)
    for hh in range(N_HEADS):
        a0 = (NAT_KA + hh) * LANES
        ka_ref[0, hh] = (pn[:, a0:a0 + LANES] + faug[:, hh * LANES:(hh + 1) * LANES]).astype(BF16)
        b0 = (NAT_KB + hh) * LANES
        kb_ref[0, hh] = (pn[:, b0:b0 + LANES] + pos).astype(BF16)
    ki_ref[0] = pn[:, NAT_KI * LANES:(NAT_KI + 1) * LANES].astype(BF16)

    def put(group, val):
        for cc in range(tm // TILE):
            t_ref[0, cc, group * GROUP_W:(group + 1) * GROUP_W, :] = (
                val[:, cc * TILE:(cc + 1) * TILE].astype(BF16))

    def rows(group):
        return pt[group * GROUP_W:(group + 1) * GROUP_W, :]

    def silu(v):
        return v / (1.0 + jnp.exp(-v))

    put(G_QA, rows(G_QA) * q_scale)
    put(G_VA, rows(G_VA))
    put(G_GA, silu(rows(G_GA)))
    put(G_QB, rows(G_QB) * q_scale)
    put(G_VB, rows(G_VB))
    put(G_GB, silu(rows(G_GB)))
    put(G_QI, rows(G_QI))
    wi_ref[0] = pt[N_TGROUPS * GROUP_W:N_TGROUPS * GROUP_W + N_HEADS, :]


def _proj_call(x, mod3, gain, w_nat, w_t, bf_row, pos_aug, e_mats, *, tm):
    b, s, _ = x.shape
    ns = s // tm
    nt = s // TILE
    t_rows = N_TGROUPS * GROUP_W
    kern = functools.partial(
        _proj_kernel, tm=tm,
        q_scale=float(HEAD_DIM ** -0.5 * LOG2E))
    const = dict(pipeline_mode=pl.Buffered(1))
    return pl.pallas_call(
        kern,
        grid=(b, ns),
        in_specs=[
            pl.BlockSpec((1, tm, D_MODEL), lambda i, j: (i, j, 0)),
            pl.BlockSpec((1, 3, D_MODEL), lambda i, j: (i, 0, 0)),
            pl.BlockSpec((1, D_MODEL), lambda i, j: (0, 0)),
            pl.BlockSpec((D_MODEL, N_NAT), lambda i, j: (0, 0), **const),
            pl.BlockSpec((t_rows + WI_ROWS, D_MODEL), lambda i, j: (0, 0), **const),
            pl.BlockSpec((1, LANES), lambda i, j: (0, 0)),
            pl.BlockSpec((tm, LANES), lambda i, j: (j, 0)),
            pl.BlockSpec((3, LANES, N_HEADS * LANES), lambda i, j: (0, 0, 0), **const),
        ],
        out_specs=[
            pl.BlockSpec((1, N_HEADS, tm, LANES), lambda i, j: (i, 0, j, 0)),
            pl.BlockSpec((1, N_HEADS, tm, LANES), lambda i, j: (i, 0, j, 0)),
            pl.BlockSpec((1, tm, LANES), lambda i, j: (i, j, 0)),
            pl.BlockSpec((1, tm // TILE, t_rows, TILE), lambda i, j: (i, j, 0, 0)),
            pl.BlockSpec((1, N_HEADS, tm), lambda i, j: (i, 0, j)),
        ],
        out_shape=[
            jax.ShapeDtypeStruct((b, N_HEADS, s, LANES), BF16),
            jax.ShapeDtypeStruct((b, N_HEADS, s, LANES), BF16),
            jax.ShapeDtypeStruct((b, s, LANES), BF16),
            jax.ShapeDtypeStruct((b, nt, t_rows, TILE), BF16),
            jax.ShapeDtypeStruct((b, N_HEADS, s), F32),
        ],
        scratch_shapes=[pltpu.VMEM((1, LANES), F32)],
        compiler_params=pltpu.CompilerParams(
            dimension_semantics=("arbitrary", "arbitrary"), vmem_limit_bytes=VMEM_LIMIT),
        name="proj",
    )(x, mod3, gain, w_nat, w_t, bf_row, pos_aug, e_mats)


def _attend(n_last, hb, scores, adjust, v_tile, s_scr, p_scr, acc_scr, slot0=0):
    t = TILE
    for j in range(hb):
        s_scr[slot0 + j] = scores(j, 0)
        p_scr[slot0 + j] = jnp.zeros((t, t), BF16)
        acc_scr[slot0 + j] = jnp.zeros((HEAD_DIM, t), F32)

    def pv(j, ki, alpha):
        acc_scr[slot0 + j] = alpha * acc_scr[slot0 + j] + _dot(v_tile(j, ki), p_scr[slot0 + j])

    def softmax_update(j, s, m, l):
        m_new = jnp.maximum(m, jnp.max(s, axis=0, keepdims=True))
        p = jnp.exp2(s - m_new)
        alpha = jnp.exp2(m - m_new)
        p_scr[slot0 + j] = p.astype(BF16)
        return m_new, alpha * l + jnp.sum(p, axis=0, keepdims=True), alpha

    def body(ki, carry):
        out = []
        for j in range(hb):
            m, l, alpha = carry[j]
            pv(j, jnp.maximum(ki - 1, 0), alpha)
            out.append(softmax_update(j, adjust(j, ki, s_scr[slot0 + j], False), m, l))
            s_scr[slot0 + j] = scores(j, ki + 1)
        return tuple(out)

    init = tuple((jnp.full((1, t), NEG, F32), jnp.zeros((1, t), F32), jnp.ones((1, t), F32))
                 for _ in range(hb))
    carry = lax.fori_loop(0, n_last, body, init)
    denoms = []
    for j in range(hb):
        m, l, alpha = carry[j]
        pv(j, jnp.maximum(n_last - 1, 0), alpha)
        m, l, alpha = softmax_update(j, adjust(j, n_last, s_scr[slot0 + j], True), m, l)
        pv(j, n_last, alpha)
        denoms.append(l)
    return denoms


def _attend_scratch(hb):
    return [pltpu.VMEM((hb, TILE, TILE), F32),
            pltpu.VMEM((hb, TILE, TILE), BF16),
            pltpu.VMEM((hb, HEAD_DIM, TILE), F32)]


def _fox_kernel(q_ref, k_ref, v_ref, g_ref, qaug_ref, o_ref, s_scr, p_scr, acc_scr, *, hb):
    qi = pl.program_id(2)
    t = TILE
    qz = [jnp.concatenate([q_ref[0, 0, j * HEAD_DIM:(j + 1) * HEAD_DIM, :], qaug_ref[...]], axis=0)
          for j in range(hb)]
    causal = lax.broadcasted_iota(I32, (t, t), 0) <= lax.broadcasted_iota(I32, (t, t), 1)

    def scores(j, ki):
        k_t = k_ref[0, j, pl.ds(pl.multiple_of(ki * t, t), t), :]
        return _dot(k_t, qz[j])

    def adjust(j, ki, s, last):
        return jnp.where(causal, s, NEG) if last else s

    def v_tile(j, ki):
        return v_ref[0, ki, j * HEAD_DIM:(j + 1) * HEAD_DIM, :]

    denoms = _attend(qi, hb, scores, adjust, v_tile, s_scr, p_scr, acc_scr)
    for j in range(hb):
        gate = g_ref[0, 0, j * HEAD_DIM:(j + 1) * HEAD_DIM, :].astype(F32)
        o_ref[0, 0, j * HEAD_DIM:(j + 1) * HEAD_DIM, :] = ((acc_scr[j] / denoms[j]) * gate).astype(BF16)


def _fox_call(ka, t_arr, qaug, *, hb):
    b, _, s, _ = ka.shape
    nt = s // TILE
    gpb = GROUP_W // (HEAD_DIM * hb)
    rb = HEAD_DIM * hb
    return pl.pallas_call(
        functools.partial(_fox_kernel, hb=hb),
        grid=(b, N_HEADS // hb, nt),
        in_specs=[
            pl.BlockSpec((1, 1, rb, TILE), lambda i, h, q: (i, q, G_QA * gpb + h, 0)),
            pl.BlockSpec((1, hb, s, LANES), lambda i, h, q: (i, h, 0, 0)),
            pl.BlockSpec((1, nt, rb, TILE), lambda i, h, q: (i, 0, G_VA * gpb + h, 0)),
            pl.BlockSpec((1, 1, rb, TILE), lambda i, h, q: (i, q, G_GA * gpb + h, 0)),
            pl.BlockSpec((HEAD_DIM, TILE), lambda i, h, q: (0, 0)),
        ],
        out_specs=pl.BlockSpec((1, 1, rb, TILE), lambda i, h, q: (i, q, h, 0)),
        out_shape=jax.ShapeDtypeStruct((b, nt, GROUP_W, TILE), BF16),
        scratch_shapes=_attend_scratch(hb),
        compiler_params=pltpu.CompilerParams(
            dimension_semantics=("arbitrary", "arbitrary", "arbitrary"),
            vmem_limit_bytes=VMEM_LIMIT),
        name="fox",
    )(t_arr, ka, t_arr, t_arr, qaug)


def _dsa_kernel(qi_ref, wi_ref, ki_ref, q_ref, k_ref, v_ref, g_ref, qaug_ref, o_ref,
                key_ref, bias_ref, thr_ref, lim_ref, s_scr, p_scr, acc_scr,
                *, top_k, idx_bits, idx_scale, hb):
    qidx = pl.program_id(1)
    t = TILE
    nkv = qidx + 1
    srow = lax.broadcasted_iota(I32, (t, t), 0)
    tcol = lax.broadcasted_iota(I32, (t, t), 1)
    zpad = jnp.zeros((LANES - HEAD_DIM, t), BF16)

    qiz = [jnp.concatenate([qi_ref[0, 0, j * HEAD_DIM:(j + 1) * HEAD_DIM, :], zpad], axis=0)
           for j in range(N_HEADS)]
    wrow = [wi_ref[0, j:j + 1, :] for j in range(N_HEADS)]

    def score_keys(ki):
        k_t = ki_ref[0, pl.ds(pl.multiple_of(ki * t, t), t), :]
        sc = jnp.zeros((t, t), F32)
        for j in range(N_HEADS):
            sc = sc + wrow[j] * jnp.maximum(_dot(k_t, qiz[j]), 0.0)
        bits = lax.bitcast_convert_type(sc * idx_scale, I32)
        return bits ^ ((bits >> 31) & 0x7FFFFFFF)

    def p1(ki, c):
        key_ref[ki] = score_keys(ki)
        return c

    lax.fori_loop(0, qidx, p1, 0)
    key_ref[qidx] = jnp.where(srow <= tcol, score_keys(qidx), INT_MIN)

    def count(pred):
        def body(ki, c):
            sidx = srow + ki * t
            ind = jnp.where(pred(key_ref[ki], sidx), 1, 0).astype(I32)
            return c + jnp.sum(ind.reshape(t // 8, 8, t), axis=0)
        c8 = lax.fori_loop(0, nkv, body, jnp.zeros((8, t), I32))
        return jnp.sum(c8, axis=0, keepdims=True)

    thr_ref[...] = jnp.full((1, t), INT_MIN, I32)
    lim_ref[...] = jnp.zeros((1, t), I32)

    @pl.when(nkv * t > top_k)
    def _():
        zero = jnp.zeros((1, t), I32)
        c0 = count(lambda kk, sidx: kk >= zero)
        thr0 = jnp.where(c0 >= top_k, 0, INT_MIN).astype(I32)

        def bit_body(i, thr):
            cand = thr + lax.shift_left(jnp.int32(1), 30 - i)
            c = count(lambda kk, sidx: kk >= cand)
            return jnp.where(c >= top_k, cand, thr)

        thr = lax.fori_loop(0, 31, bit_body, thr0)
        thr_ref[...] = thr

        active = thr > INT_MIN
        c_ge = count(lambda kk, sidx: kk >= thr)
        c_gt = count(lambda kk, sidx: kk > thr)
        need = top_k - c_gt
        big = jnp.int32(2 ** idx_bits - 1)
        lim_ref[...] = jnp.where(active, big, 0)
        excess = jnp.max(jnp.where(active & (c_ge > top_k), 1, 0))

        @pl.when(excess > 0)
        def _():
            def lim_body(i, lim):
                cand = lim + lax.shift_left(jnp.int32(1), idx_bits - 1 - i)
                c = count(lambda kk, sidx: (kk == thr) & (sidx < cand))
                return jnp.where(c <= need, cand, lim)

            lim = lax.fori_loop(0, idx_bits, lim_body, jnp.zeros((1, t), I32))
            lim_ref[...] = jnp.where(active, lim, 0)

    thr = thr_ref[...]
    lim = lim_ref[...]

    def p3(ki, c):
        kk = key_ref[ki]
        sidx = srow + ki * t
        sel = (kk > thr) | ((kk == thr) & (sidx < lim))
        bias_ref[ki] = jnp.where(sel, 0.0, NEG).astype(F32)
        return c

    lax.fori_loop(0, nkv, p3, 0)

    for j0 in range(0, N_HEADS, hb):
        qz = [jnp.concatenate([q_ref[0, 0, j * HEAD_DIM:(j + 1) * HEAD_DIM, :], qaug_ref[j]], axis=0)
              for j in range(j0, j0 + hb)]

        def scores(n, ki, j0=j0, qz=qz):
            k_t = k_ref[0, j0 + n, pl.ds(pl.multiple_of(ki * t, t), t), :]
            return _dot(k_t, qz[n])

        def adjust(n, ki, s, last):
            return s + bias_ref[ki]

        def v_tile(n, ki, j0=j0):
            return v_ref[0, ki, (j0 + n) * HEAD_DIM:(j0 + n + 1) * HEAD_DIM, :]

        denoms = _attend(qidx, hb, scores, adjust, v_tile, s_scr, p_scr, acc_scr, slot0=j0)
        for n in range(hb):
            j = j0 + n
            gate = g_ref[0, 0, j * HEAD_DIM:(j + 1) * HEAD_DIM, :].astype(F32)
            o_ref[0, 0, j * HEAD_DIM:(j + 1) * HEAD_DIM, :] = ((acc_scr[j] / denoms[n]) * gate).astype(BF16)


def _dsa_call(t_arr, wi, ki, kb, qaug, *, hb):
    b, _, s, _ = kb.shape
    nt = s // TILE
    top_k = min(TOPK_MAX, s // 4)
    idx_bits = int(np.ceil(np.log2(s))) + 1
    kern = functools.partial(_dsa_kernel, top_k=top_k, idx_bits=idx_bits, hb=hb,
                             idx_scale=float(HEAD_DIM ** -0.5 * N_HEADS ** -0.5))
    return pl.pallas_call(
        kern,
        grid=(b, nt),
        in_specs=[
            pl.BlockSpec((1, 1, GROUP_W, TILE), lambda i, q: (i, q, G_QI, 0)),
            pl.BlockSpec((1, N_HEADS, TILE), lambda i, q: (i, 0, q)),
            pl.BlockSpec((1, s, LANES), lambda i, q: (i, 0, 0)),
            pl.BlockSpec((1, 1, GROUP_W, TILE), lambda i, q: (i, q, G_QB, 0)),
            pl.BlockSpec((1, N_HEADS, s, LANES), lambda i, q: (i, 0, 0, 0)),
            pl.BlockSpec((1, nt, GROUP_W, TILE), lambda i, q: (i, 0, G_VB, 0)),
            pl.BlockSpec((1, 1, GROUP_W, TILE), lambda i, q: (i, q, G_GB, 0)),
            pl.BlockSpec((N_HEADS, HEAD_DIM, TILE), lambda i, q: (0, 0, 0)),
        ],
        out_specs=pl.BlockSpec((1, 1, GROUP_W, TILE), lambda i, q: (i, q, 0, 0)),
        out_shape=jax.ShapeDtypeStruct((b, nt, GROUP_W, TILE), BF16),
        scratch_shapes=[
            pltpu.VMEM((nt, TILE, TILE), I32),
            pltpu.VMEM((nt, TILE, TILE), F32),
            pltpu.VMEM((1, TILE), I32),
            pltpu.VMEM((1, TILE), I32),
        ] + _attend_scratch(N_HEADS),
        compiler_params=pltpu.CompilerParams(
            dimension_semantics=("arbitrary", "arbitrary"), vmem_limit_bytes=VMEM_LIMIT),
        name="dsa",
    )(t_arr, wi, ki, t_arr, kb, t_arr, t_arr, qaug)


def _out_kernel(x_ref, ya_ref, yb_ref, w_ref, gate_ref, gain_ref, o_ref):
    y_t = jnp.concatenate([ya_ref[0, 0], yb_ref[0, 0]], axis=0)
    y = _dot_tn(y_t, w_ref[...])
    z = x_ref[0] + gate_ref[0] * y
    ms = jnp.mean(z * z, axis=-1, keepdims=True)
    o_ref[0] = (z * lax.rsqrt(ms + EPS)) * gain_ref[...]


def _out_call(x, ya, yb, w_out, gate3, final_gain):
    b, s, _ = x.shape
    nt = s // TILE
    return pl.pallas_call(
        _out_kernel,
        grid=(b, nt),
        in_specs=[
            pl.BlockSpec((1, TILE, D_MODEL), lambda i, q: (i, q, 0)),
            pl.BlockSpec((1, 1, GROUP_W, TILE), lambda i, q: (i, q, 0, 0)),
            pl.BlockSpec((1, 1, GROUP_W, TILE), lambda i, q: (i, q, 0, 0)),
            pl.BlockSpec((2 * GROUP_W, D_MODEL), lambda i, q: (0, 0)),
            pl.BlockSpec((1, 1, D_MODEL), lambda i, q: (i, 0, 0)),
            pl.BlockSpec((1, D_MODEL), lambda i, q: (0, 0)),
        ],
        out_specs=pl.BlockSpec((1, TILE, D_MODEL), lambda i, q: (i, q, 0)),
        out_shape=jax.ShapeDtypeStruct((b, s, D_MODEL), F32),
        compiler_params=pltpu.CompilerParams(
            dimension_semantics=("arbitrary", "arbitrary"), vmem_limit_bytes=VMEM_LIMIT),
        name="outproj",
    )(x, ya, yb, w_out, gate3, final_gain)


def _layout_weights(w_in):
    widths = (GROUP_W, GROUP_W, GROUP_W, N_HEADS, GROUP_W, GROUP_W, GROUP_W, GROUP_W,
              N_HEADS * HEAD_DIM, HEAD_DIM, N_HEADS, GROUP_W)
    offs = np.concatenate([[0], np.cumsum(widths)])
    (w_qa, w_ka, w_va, w_f, w_ga, w_qb, w_kb, w_vb, w_qi, w_ki, w_wi, w_gb) = (
        w_in[:, offs[n]:offs[n + 1]] for n in range(len(widths)))

    def pad_heads(w):
        w = w.reshape(D_MODEL, N_HEADS, HEAD_DIM)
        return jnp.pad(w, ((0, 0), (0, 0), (0, LANES - HEAD_DIM))).reshape(D_MODEL, N_HEADS * LANES)

    def pad_cols(w):
        return jnp.pad(w, ((0, 0), (0, LANES - w.shape[1])))

    w_nat = jnp.concatenate([pad_heads(w_ka), pad_heads(w_kb), pad_cols(w_f), pad_cols(w_ki)], axis=1)
    w_t = jnp.concatenate([w_qa, w_va, w_ga, w_qb, w_vb, w_gb, w_qi,
                           jnp.pad(w_wi, ((0, 0), (0, WI_ROWS - N_HEADS)))], axis=1).T
    return w_nat.astype(BF16), w_t.astype(BF16)


def _constants(s):
    e = np.zeros((3, LANES, N_HEADS * LANES), np.float32)
    for i in range(3):
        for h in range(N_HEADS):
            e[i, h, h * LANES + HEAD_DIM + i] = 1.0
    pos = np.zeros((s, LANES), np.float32)
    sidx = np.arange(s)
    for i in range(3):
        pos[:, HEAD_DIM + i] = sidx // 64
        pos[:, HEAD_DIM + 3 + i] = sidx % 64
    qa = np.zeros((HEAD_DIM, TILE), np.float32)
    qa[0:3, :] = 1.0
    return jnp.asarray(e, BF16), jnp.asarray(pos, BF16), jnp.asarray(qa, BF16)


def _alibi_rows():
    i = jnp.arange(1, N_HEADS + 1, dtype=F32)
    slopes = jnp.exp2(-8.0 * i / N_HEADS) * LOG2E
    c64 = _split3(slopes * 64.0)
    c1 = _split3(slopes)
    rows = jnp.stack(list(c64) + list(c1), axis=1)
    rows = jnp.pad(rows, ((0, 0), (0, HEAD_DIM - 6)))
    return jnp.broadcast_to(rows[:, :, None], (N_HEADS, HEAD_DIM, TILE)).astype(BF16)


def kernel(x, c, w_mod, b_mod, norm_gain, w_in, b_forget, w_out, final_gain):
    b, s, d = x.shape
    assert d == D_MODEL and s % TILE == 0
    assert w_mod.shape[0] == 1, "single-layer block"
    tm = TILE

    mod = _mod_call(c, w_mod[0], b_mod[0])
    mod3 = mod.reshape(b, 3, D_MODEL)
    w_nat, w_t = _layout_weights(w_in[0])
    e_mats, pos_aug, qaug_a = _constants(s)
    bf_row = jnp.pad(b_forget[0], (0, LANES - N_HEADS)).reshape(1, LANES)

    ka, kb, ki, t_arr, wi = _proj_call(
        x, mod3, norm_gain[0].reshape(1, D_MODEL), w_nat, w_t, bf_row, pos_aug, e_mats, tm=tm)

    ya = _fox_call(ka, t_arr, qaug_a, hb=HEADS_PER_BODY)
    yb = _dsa_call(t_arr, wi, ki, kb, _alibi_rows(), hb=HEADS_PER_BODY)

    gate3 = mod3[:, 2:3, :]
    return _out_call(x, ya, yb, w_out[0].astype(BF16), gate3, final_gain.reshape(1, D_MODEL))
```

```python
import functools

import jax
import jax.numpy as jnp
import numpy as np
from jax import lax
from jax.experimental import pallas as pl
from jax.experimental.pallas import tpu as pltpu

F32 = jnp.float32
BF16 = jnp.bfloat16
I32 = jnp.int32

D_MODEL = 1024
HEAD_DIM = 64
N_HEADS = 8
GROUP_W = N_HEADS * HEAD_DIM
TOPK_MAX = 256
EPS = 1e-6
LOG2E = 1.4426950408889634
NEG = -1e30
INT_MIN = -(2 ** 31)

LANES = 128
TILE = 256
HEADS_PER_BODY = 8
ACC_ROWS = HEAD_DIM + 16
VMEM_LIMIT = 56 * 1024 * 1024

G_QA, G_VA, G_GA, G_QB, G_VB, G_GB, G_QI = range(7)
N_TGROUPS = 7
WI_ROWS = 16
NAT_KA = 0
NAT_KB = N_HEADS
NAT_F = 2 * N_HEADS
NAT_KI = 2 * N_HEADS + 1
N_NAT = (2 * N_HEADS + 2) * LANES


def _split3(v):
    hi = v.astype(BF16)
    r1 = v - hi.astype(F32)
    mid = r1.astype(BF16)
    r2 = r1 - mid.astype(F32)
    return hi, mid, r2.astype(BF16)


def _dot(a, b):
    return jnp.dot(a, b, preferred_element_type=F32)


def _dot_nt(a, b):
    return lax.dot_general(a, b, (((1,), (1,)), ((), ())), preferred_element_type=F32)


def _dot_tn(a, b):
    return lax.dot_general(a, b, (((0,), (0,)), ((), ())), preferred_element_type=F32)


def _mod_kernel(c_ref, w_ref, b_ref, o_ref):
    c = c_ref[...]
    act = c / (1.0 + jnp.exp(-c))
    a_hi = act.astype(BF16)
    a_lo = (act - a_hi.astype(F32)).astype(BF16)
    w = w_ref[...]
    w_hi = w.astype(BF16)
    w_lo = (w - w_hi.astype(F32)).astype(BF16)
    o_ref[...] = _dot(a_hi, w_hi) + _dot(a_hi, w_lo) + _dot(a_lo, w_hi) + b_ref[...]


def _mod_call(c, w_mod, b_mod):
    b = c.shape[0]
    n = w_mod.shape[1]
    bn = D_MODEL
    return pl.pallas_call(
        _mod_kernel,
        grid=(n // bn,),
        in_specs=[
            pl.BlockSpec((b, D_MODEL), lambda j: (0, 0)),
            pl.BlockSpec((D_MODEL, bn), lambda j: (0, j)),
            pl.BlockSpec((1, bn), lambda j: (0, j)),
        ],
        out_specs=pl.BlockSpec((b, bn), lambda j: (0, j)),
        out_shape=jax.ShapeDtypeStruct((b, n), F32),
        compiler_params=pltpu.CompilerParams(dimension_semantics=("arbitrary",)),
        name="mod",
    )(c, w_mod, b_mod.reshape(1, n))


def _proj_kernel(x_ref, mod_ref, gain_ref, wnat_ref, wt_ref, bf_ref, pos_ref, e_ref,
                 ka_ref, kb_ref, ki_ref, t_ref, wi_ref, carry_ref, *, tm, q_scale):
    si = pl.program_id(1)

    @pl.when(si == 0)
    def _():
        carry_ref[...] = jnp.zeros_like(carry_ref)

    x = x_ref[0]
    ms = jnp.mean(x * x, axis=-1, keepdims=True)
    xn = (x * lax.rsqrt(ms + EPS)) * gain_ref[...]
    shift = mod_ref[0, 0:1, :]
    scale = mod_ref[0, 1:2, :]
    h = (xn * (1.0 + scale) + shift).astype(BF16)

    pn = _dot(h, wnat_ref[...])
    pt = _dot_nt(wt_ref[...], h)

    lane = lax.broadcasted_iota(I32, (tm, LANES), 1)
    fl = pn[:, NAT_F * LANES:(NAT_F + 1) * LANES] + bf_ref[...]
    logf = jnp.minimum(fl, 0.0) - jnp.log(1.0 + jnp.exp(-jnp.abs(fl)))
    logf = jnp.where(lane < N_HEADS, logf, 0.0)
    row = lax.broadcasted_iota(I32, (tm, tm), 0)
    col = lax.broadcasted_iota(I32, (tm, tm), 1)
    tri = jnp.where(col <= row, 1.0, 0.0).astype(BF16)
    l_hi, l_mid, l_lo = _split3(logf)
    cum = carry_ref[...] + (_dot(tri, l_hi) + _dot(tri, l_mid) + _dot(tri, l_lo))
    carry_ref[...] = cum[tm - 1:tm, :]

    f_hi, f_mid, f_lo = _split3(cum * (-LOG2E))
    faug = _dot(f_hi, e_ref[0]) + _dot(f_mid, e_ref[1]) + _dot(f_lo, e_ref[2])
    pos = pos_ref[...].astype(F32)
    for hh in range(N_HEADS):
        a0 = (NAT_KA + hh) * LANES
        ka_ref[0, hh] = (pn[:, a0:a0 + LANES] + faug[:, hh * LANES:(hh + 1) * LANES]).astype(BF16)
        b0 = (NAT_KB + hh) * LANES
        kb_ref[0, hh] = (pn[:, b0:b0 + LANES] + pos).astype(BF16)
    ki_ref[0] = pn[:, NAT_KI * LANES:(NAT_KI + 1) * LANES].astype(BF16)

    def put(group, val):
        for cc in range(tm // TILE):
            t_ref[0, cc, group * GROUP_W:(group + 1) * GROUP_W, :] = (
                val[:, cc * TILE:(cc + 1) * TILE].astype(BF16))

    def rows(group):
        return pt[group * GROUP_W:(group + 1) * GROUP_W, :]

    def silu(v):
        return v / (1.0 + jnp.exp(-v))

    put(G_QA, rows(G_QA) * q_scale)
    put(G_VA, rows(G_VA))
    put(G_GA, silu(rows(G_GA)))
    put(G_QB, rows(G_QB) * q_scale)
    put(G_VB, rows(G_VB))
    put(G_GB, silu(rows(G_GB)))
    put(G_QI, rows(G_QI))
    wi_ref[0] = pt[N_TGROUPS * GROUP_W:N_TGROUPS * GROUP_W + N_HEADS, :]


def _proj_call(x, mod3, gain, w_nat, w_t, bf_row, pos_aug, e_mats, *, tm):
    b, s, _ = x.shape
    ns = s // tm
    nt = s // TILE
    t_rows = N_TGROUPS * GROUP_W
    kern = functools.partial(
        _proj_kernel, tm=tm,
        q_scale=float(HEAD_DIM ** -0.5 * LOG2E))
    const = dict(pipeline_mode=pl.Buffered(1))
    return pl.pallas_call(
        kern,
        grid=(b, ns),
        in_specs=[
            pl.BlockSpec((1, tm, D_MODEL), lambda i, j: (i, j, 0)),
            pl.BlockSpec((1, 3, D_MODEL), lambda i, j: (i, 0, 0)),
            pl.BlockSpec((1, D_MODEL), lambda i, j: (0, 0)),
            pl.BlockSpec((D_MODEL, N_NAT), lambda i, j: (0, 0), **const),
            pl.BlockSpec((t_rows + WI_ROWS, D_MODEL), lambda i, j: (0, 0), **const),
            pl.BlockSpec((1, LANES), lambda i, j: (0, 0)),
            pl.BlockSpec((tm, LANES), lambda i, j: (j, 0)),
            pl.BlockSpec((3, LANES, N_HEADS * LANES), lambda i, j: (0, 0, 0), **const),
        ],
        out_specs=[
            pl.BlockSpec((1, N_HEADS, tm, LANES), lambda i, j: (i, 0, j, 0)),
            pl.BlockSpec((1, N_HEADS, tm, LANES), lambda i, j: (i, 0, j, 0)),
            pl.BlockSpec((1, tm, LANES), lambda i, j: (i, j, 0)),
            pl.BlockSpec((1, tm // TILE, t_rows, TILE), lambda i, j: (i, j, 0, 0)),
            pl.BlockSpec((1, N_HEADS, tm), lambda i, j: (i, 0, j)),
        ],
        out_shape=[
            jax.ShapeDtypeStruct((b, N_HEADS, s, LANES), BF16),
            jax.ShapeDtypeStruct((b, N_HEADS, s, LANES), BF16),
            jax.ShapeDtypeStruct((b, s, LANES), BF16),
            jax.ShapeDtypeStruct((b, nt, t_rows, TILE), BF16),
            jax.ShapeDtypeStruct((b, N_HEADS, s), F32),
        ],
        scratch_shapes=[pltpu.VMEM((1, LANES), F32)],
        compiler_params=pltpu.CompilerParams(
            dimension_semantics=("arbitrary", "arbitrary"), vmem_limit_bytes=VMEM_LIMIT),
        name="proj",
    )(x, mod3, gain, w_nat, w_t, bf_row, pos_aug, e_mats)


def _attend(n_last, hb, scores, mask_last, v_tile, s_scr, p_scr, acc_scr, slot0=0):
    t = TILE
    ones_rows = jnp.ones((ACC_ROWS - HEAD_DIM, t), BF16)

    def load_scores(j, ki):
        s = scores(j, ki)
        s_scr[slot0 + j] = s
        return jnp.max(s, axis=0, keepdims=True)

    def pv(j, ki, alpha):
        v_aug = jnp.concatenate([v_tile(j, ki), ones_rows], axis=0)
        acc_scr[slot0 + j] = alpha * acc_scr[slot0 + j] + _dot(v_aug, p_scr[slot0 + j])

    def exp_tile(j, s, m, tile_max):
        m_new = jnp.maximum(m, tile_max)
        p_scr[slot0 + j] = jnp.exp2(s - m_new).astype(BF16)
        return m_new, jnp.exp2(m - m_new)

    init = []
    for j in range(hb):
        tile_max = load_scores(j, 0)
        p_scr[slot0 + j] = jnp.zeros((t, t), BF16)
        acc_scr[slot0 + j] = jnp.zeros((ACC_ROWS, t), F32)
        init.append((jnp.full((1, t), NEG, F32), jnp.ones((1, t), F32), tile_max))

    def body(ki, carry):
        out = []
        for j in range(hb):
            m, alpha, tile_max = carry[j]
            pv(j, jnp.maximum(ki - 1, 0), alpha)
            m, alpha = exp_tile(j, s_scr[slot0 + j], m, tile_max)
            out.append((m, alpha, load_scores(j, ki + 1)))
        return tuple(out)

    carry = lax.fori_loop(0, n_last, body, tuple(init))
    for j in range(hb):
        m, alpha, _ = carry[j]
        pv(j, jnp.maximum(n_last - 1, 0), alpha)
        s = mask_last(s_scr[slot0 + j])
        m, alpha = exp_tile(j, s, m, jnp.max(s, axis=0, keepdims=True))
        pv(j, n_last, alpha)


def _attend_scratch(hb):
    return [pltpu.VMEM((hb, TILE, TILE), F32),
            pltpu.VMEM((hb, TILE, TILE), BF16),
            pltpu.VMEM((hb, ACC_ROWS, TILE), F32)]


def _attend_result(acc_scr, slot, gate):
    acc = acc_scr[slot]
    return ((acc[:HEAD_DIM] / acc[HEAD_DIM:HEAD_DIM + 1]) * gate.astype(F32)).astype(BF16)


def _fox_kernel(q_ref, k_ref, v_ref, g_ref, qaug_ref, o_ref, s_scr, p_scr, acc_scr, *, hb):
    qi = pl.program_id(2)
    t = TILE
    qz = [jnp.concatenate([q_ref[0, 0, j * HEAD_DIM:(j + 1) * HEAD_DIM, :], qaug_ref[...]], axis=0)
          for j in range(hb)]
    causal = lax.broadcasted_iota(I32, (t, t), 0) <= lax.broadcasted_iota(I32, (t, t), 1)

    def scores(j, ki):
        k_t = k_ref[0, j, pl.ds(pl.multiple_of(ki * t, t), t), :]
        return _dot(k_t, qz[j])

    def mask_last(s):
        return jnp.where(causal, s, NEG)

    def v_tile(j, ki):
        return v_ref[0, ki, j * HEAD_DIM:(j + 1) * HEAD_DIM, :]

    _attend(qi, hb, scores, mask_last, v_tile, s_scr, p_scr, acc_scr)
    for j in range(hb):
        rows = slice(j * HEAD_DIM, (j + 1) * HEAD_DIM)
        o_ref[0, 0, rows, :] = _attend_result(acc_scr, j, g_ref[0, 0, rows, :])


def _fox_call(ka, t_arr, qaug, *, hb):
    b, _, s, _ = ka.shape
    nt = s // TILE
    gpb = GROUP_W // (HEAD_DIM * hb)
    rb = HEAD_DIM * hb
    return pl.pallas_call(
        functools.partial(_fox_kernel, hb=hb),
        grid=(b, N_HEADS // hb, nt),
        in_specs=[
            pl.BlockSpec((1, 1, rb, TILE), lambda i, h, q: (i, q, G_QA * gpb + h, 0)),
            pl.BlockSpec((1, hb, s, LANES), lambda i, h, q: (i, h, 0, 0)),
            pl.BlockSpec((1, nt, rb, TILE), lambda i, h, q: (i, 0, G_VA * gpb + h, 0)),
            pl.BlockSpec((1, 1, rb, TILE), lambda i, h, q: (i, q, G_GA * gpb + h, 0)),
            pl.BlockSpec((HEAD_DIM, TILE), lambda i, h, q: (0, 0)),
        ],
        out_specs=pl.BlockSpec((1, 1, rb, TILE), lambda i, h, q: (i, q, h, 0)),
        out_shape=jax.ShapeDtypeStruct((b, nt, GROUP_W, TILE), BF16),
        scratch_shapes=_attend_scratch(hb),
        compiler_params=pltpu.CompilerParams(
            dimension_semantics=("arbitrary", "arbitrary", "arbitrary"),
            vmem_limit_bytes=VMEM_LIMIT),
        name="fox",
    )(t_arr, ka, t_arr, t_arr, qaug)


def _dsa_kernel(qi_ref, wi_ref, ki_ref, q_ref, k_ref, v_ref, g_ref, qaug_ref, o_ref,
                key_ref, bias_ref, thr_ref, lim_ref, s_scr, p_scr, acc_scr,
                *, top_k, idx_bits, idx_scale, hb):
    qidx = pl.program_id(1)
    t = TILE
    nkv = qidx + 1
    srow = lax.broadcasted_iota(I32, (t, t), 0)
    tcol = lax.broadcasted_iota(I32, (t, t), 1)
    zpad = jnp.zeros((LANES - HEAD_DIM, t), BF16)

    qiz = [jnp.concatenate([qi_ref[0, 0, j * HEAD_DIM:(j + 1) * HEAD_DIM, :], zpad], axis=0)
           for j in range(N_HEADS)]
    wrow = [wi_ref[0, j:j + 1, :] for j in range(N_HEADS)]

    def score_keys(ki):
        k_t = ki_ref[0, pl.ds(pl.multiple_of(ki * t, t), t), :]
        sc = jnp.zeros((t, t), F32)
        for j in range(N_HEADS):
            sc = sc + wrow[j] * jnp.maximum(_dot(k_t, qiz[j]), 0.0)
        bits = lax.bitcast_convert_type(sc * idx_scale, I32)
        return bits ^ ((bits >> 31) & 0x7FFFFFFF)

    def p1(ki, c):
        key_ref[ki] = score_keys(ki)
        return c

    lax.fori_loop(0, qidx, p1, 0)
    key_ref[qidx] = jnp.where(srow <= tcol, score_keys(qidx), INT_MIN)

    def count(pred):
        def body(ki, c):
            sidx = srow + ki * t
            ind = jnp.where(pred(key_ref[ki], sidx), 1, 0).astype(I32)
            return c + jnp.sum(ind.reshape(t // 8, 8, t), axis=0)
        c8 = lax.fori_loop(0, nkv, body, jnp.zeros((8, t), I32))
        return jnp.sum(c8, axis=0, keepdims=True)

    thr_ref[...] = jnp.full((1, t), INT_MIN, I32)
    lim_ref[...] = jnp.zeros((1, t), I32)

    @pl.when(nkv * t > top_k)
    def _():
        zero = jnp.zeros((1, t), I32)
        c0 = count(lambda kk, sidx: kk >= zero)
        thr0 = jnp.where(c0 >= top_k, 0, INT_MIN).astype(I32)

        def bit_body(i, thr):
            cand = thr + lax.shift_left(jnp.int32(1), 30 - i)
            c = count(lambda kk, sidx: kk >= cand)
            return jnp.where(c >= top_k, cand, thr)

        thr = lax.fori_loop(0, 31, bit_body, thr0)
        thr_ref[...] = thr

        active = thr > INT_MIN
        c_ge = count(lambda kk, sidx: kk >= thr)
        c_gt = count(lambda kk, sidx: kk > thr)
        need = top_k - c_gt
        big = jnp.int32(2 ** idx_bits - 1)
        lim_ref[...] = jnp.where(active, big, 0)
        excess = jnp.max(jnp.where(active & (c_ge > top_k), 1, 0))

        @pl.when(excess > 0)
        def _():
            def lim_body(i, lim):
                cand = lim + lax.shift_left(jnp.int32(1), idx_bits - 1 - i)
                c = count(lambda kk, sidx: (kk == thr) & (sidx < cand))
                return jnp.where(c <= need, cand, lim)

            lim = lax.fori_loop(0, idx_bits, lim_body, jnp.zeros((1, t), I32))
            lim_ref[...] = jnp.where(active, lim, 0)

    thr = thr_ref[...]
    lim = lim_ref[...]

    def p3(ki, c):
        kk = key_ref[ki]
        sidx = srow + ki * t
        sel = (kk > thr) | ((kk == thr) & (sidx < lim))
        bias_ref[ki] = jnp.where(sel, 0.0, NEG).astype(F32)
        return c

    lax.fori_loop(0, nkv, p3, 0)

    for j0 in range(0, N_HEADS, hb):
        qz = [jnp.concatenate([q_ref[0, 0, j * HEAD_DIM:(j + 1) * HEAD_DIM, :], qaug_ref[j]], axis=0)
              for j in range(j0, j0 + hb)]

        def scores(n, ki, j0=j0, qz=qz):
            k_t = k_ref[0, j0 + n, pl.ds(pl.multiple_of(ki * t, t), t), :]
            return _dot(k_t, qz[n]) + bias_ref[ki]

        def v_tile(n, ki, j0=j0):
            return v_ref[0, ki, (j0 + n) * HEAD_DIM:(j0 + n + 1) * HEAD_DIM, :]

        _attend(qidx, hb, scores, lambda s: s, v_tile, s_scr, p_scr, acc_scr, slot0=j0)
        for j in range(j0, j0 + hb):
            rows = slice(j * HEAD_DIM, (j + 1) * HEAD_DIM)
            o_ref[0, 0, rows, :] = _attend_result(acc_scr, j, g_ref[0, 0, rows, :])


def _dsa_call(t_arr, wi, ki, kb, qaug, *, hb):
    b, _, s, _ = kb.shape
    nt = s // TILE
    top_k = min(TOPK_MAX, s // 4)
    idx_bits = int(np.ceil(np.log2(s))) + 1
    kern = functools.partial(_dsa_kernel, top_k=top_k, idx_bits=idx_bits, hb=hb,
                             idx_scale=float(HEAD_DIM ** -0.5 * N_HEADS ** -0.5))
    return pl.pallas_call(
        kern,
        grid=(b, nt),
        in_specs=[
            pl.BlockSpec((1, 1, GROUP_W, TILE), lambda i, q: (i, q, G_QI, 0)),
            pl.BlockSpec((1, N_HEADS, TILE), lambda i, q: (i, 0, q)),
            pl.BlockSpec((1, s, LANES), lambda i, q: (i, 0, 0)),
            pl.BlockSpec((1, 1, GROUP_W, TILE), lambda i, q: (i, q, G_QB, 0)),
            pl.BlockSpec((1, N_HEADS, s, LANES), lambda i, q: (i, 0, 0, 0)),
            pl.BlockSpec((1, nt, GROUP_W, TILE), lambda i, q: (i, 0, G_VB, 0)),
            pl.BlockSpec((1, 1, GROUP_W, TILE), lambda i, q: (i, q, G_GB, 0)),
            pl.BlockSpec((N_HEADS, HEAD_DIM, TILE), lambda i, q: (0, 0, 0)),
        ],
        out_specs=pl.BlockSpec((1, 1, GROUP_W, TILE), lambda i, q: (i, q, 0, 0)),
        out_shape=jax.ShapeDtypeStruct((b, nt, GROUP_W, TILE), BF16),
        scratch_shapes=[
            pltpu.VMEM((nt, TILE, TILE), I32),
            pltpu.VMEM((nt, TILE, TILE), F32),
            pltpu.VMEM((1, TILE), I32),
            pltpu.VMEM((1, TILE), I32),
        ] + _attend_scratch(N_HEADS),
        compiler_params=pltpu.CompilerParams(
            dimension_semantics=("arbitrary", "arbitrary"), vmem_limit_bytes=VMEM_LIMIT),
        name="dsa",
    )(t_arr, wi, ki, t_arr, kb, t_arr, t_arr, qaug)


def _out_kernel(x_ref, ya_ref, yb_ref, w_ref, gate_ref, gain_ref, o_ref):
    y_t = jnp.concatenate([ya_ref[0, 0], yb_ref[0, 0]], axis=0)
    y = _dot_tn(y_t, w_ref[...])
    z = x_ref[0] + gate_ref[0] * y
    ms = jnp.mean(z * z, axis=-1, keepdims=True)
    o_ref[0] = (z * lax.rsqrt(ms + EPS)) * gain_ref[...]


def _out_call(x, ya, yb, w_out, gate3, final_gain):
    b, s, _ = x.shape
    nt = s // TILE
    return pl.pallas_call(
        _out_kernel,
        grid=(b, nt),
        in_specs=[
            pl.BlockSpec((1, TILE, D_MODEL), lambda i, q: (i, q, 0)),
            pl.BlockSpec((1, 1, GROUP_W, TILE), lambda i, q: (i, q, 0, 0)),
            pl.BlockSpec((1, 1, GROUP_W, TILE), lambda i, q: (i, q, 0, 0)),
            pl.BlockSpec((2 * GROUP_W, D_MODEL), lambda i, q: (0, 0)),
            pl.BlockSpec((1, 1, D_MODEL), lambda i, q: (i, 0, 0)),
            pl.BlockSpec((1, D_MODEL), lambda i, q: (0, 0)),
        ],
        out_specs=pl.BlockSpec((1, TILE, D_MODEL), lambda i, q: (i, q, 0)),
        out_shape=jax.ShapeDtypeStruct((b, s, D_MODEL), F32),
        compiler_params=pltpu.CompilerParams(
            dimension_semantics=("arbitrary", "arbitrary"), vmem_limit_bytes=VMEM_LIMIT),
        name="outproj",
    )(x, ya, yb, w_out, gate3, final_gain)


def _layout_weights(w_in):
    widths = (GROUP_W, GROUP_W, GROUP_W, N_HEADS, GROUP_W, GROUP_W, GROUP_W, GROUP_W,
              N_HEADS * HEAD_DIM, HEAD_DIM, N_HEADS, GROUP_W)
    offs = np.concatenate([[0], np.cumsum(widths)])
    (w_qa, w_ka, w_va, w_f, w_ga, w_qb, w_kb, w_vb, w_qi, w_ki, w_wi, w_gb) = (
        w_in[:, offs[n]:offs[n + 1]] for n in range(len(widths)))

    def pad_heads(w):
        w = w.reshape(D_MODEL, N_HEADS, HEAD_DIM)
        return jnp.pad(w, ((0, 0), (0, 0), (0, LANES - HEAD_DIM))).reshape(D_MODEL, N_HEADS * LANES)

    def pad_cols(w):
        return jnp.pad(w, ((0, 0), (0, LANES - w.shape[1])))

    w_nat = jnp.concatenate([pad_heads(w_ka), pad_heads(w_kb), pad_cols(w_f), pad_cols(w_ki)], axis=1)
    w_t = jnp.concatenate([w_qa, w_va, w_ga, w_qb, w_vb, w_gb, w_qi,
                           jnp.pad(w_wi, ((0, 0), (0, WI_ROWS - N_HEADS)))], axis=1).T
    return w_nat.astype(BF16), w_t.astype(BF16)


def _constants(s):
    e = np.zeros((3, LANES, N_HEADS * LANES), np.float32)
    for i in range(3):
        for h in range(N_HEADS):
            e[i, h, h * LANES + HEAD_DIM + i] = 1.0
    pos = np.zeros((s, LANES), np.float32)
    sidx = np.arange(s)
    for i in range(3):
        pos[:, HEAD_DIM + i] = sidx // 64
        pos[:, HEAD_DIM + 3 + i] = sidx % 64
    qa = np.zeros((HEAD_DIM, TILE), np.float32)
    qa[0:3, :] = 1.0
    return jnp.asarray(e, BF16), jnp.asarray(pos, BF16), jnp.asarray(qa, BF16)


def _alibi_rows():
    i = jnp.arange(1, N_HEADS + 1, dtype=F32)
    slopes = jnp.exp2(-8.0 * i / N_HEADS) * LOG2E
    c64 = _split3(slopes * 64.0)
    c1 = _split3(slopes)
    rows = jnp.stack(list(c64) + list(c1), axis=1)
    rows = jnp.pad(rows, ((0, 0), (0, HEAD_DIM - 6)))
    return jnp.broadcast_to(rows[:, :, None], (N_HEADS, HEAD_DIM, TILE)).astype(BF16)


def kernel(x, c, w_mod, b_mod, norm_gain, w_in, b_forget, w_out, final_gain):
    b, s, d = x.shape
    assert d == D_MODEL and s % TILE == 0
    assert w_mod.shape[0] == 1, "single-layer block"
    tm = TILE

    mod = _mod_call(c, w_mod[0], b_mod[0])
    mod3 = mod.reshape(b, 3, D_MODEL)
    w_nat, w_t = _layout_weights(w_in[0])
    e_mats, pos_aug, qaug_a = _constants(s)
    bf_row = jnp.pad(b_forget[0], (0, LANES - N_HEADS)).reshape(1, LANES)

    ka, kb, ki, t_arr, wi = _proj_call(
        x, mod3, norm_gain[0].reshape(1, D_MODEL), w_nat, w_t, bf_row, pos_aug, e_mats, tm=tm)

    ya = _fox_call(ka, t_arr, qaug_a, hb=HEADS_PER_BODY)
    yb = _dsa_call(t_arr, wi, ki, kb, _alibi_rows(), hb=HEADS_PER_BODY)

    gate3 = mod3[:, 2:3, :]
    return _out_call(x, ya, yb, w_out[0].astype(BF16), gate3, final_gain.reshape(1, D_MODEL))
```

```python
import functools

import jax
import jax.numpy as jnp
import numpy as np
from jax import lax
from jax.experimental import pallas as pl
from jax.experimental.pallas import tpu as pltpu

F32 = jnp.float32
BF16 = jnp.bfloat16
I32 = jnp.int32
I16 = jnp.int16

D_MODEL = 1024
HEAD_DIM = 64
N_HEADS = 8
GROUP_W = N_HEADS * HEAD_DIM
TOPK_MAX = 256
EPS = 1e-6
LOG2E = 1.4426950408889634
NEG = -1e30
INT_MIN = -(2 ** 31)

LANES = 128
TILE = 256
ROW_BLOCK = 512
HEADS_PER_BODY = 8
ACC_ROWS = HEAD_DIM + 16
VMEM_LIMIT = 56 * 1024 * 1024

G_QA, G_VA, G_GA, G_QB, G_VB, G_GB, G_QI = range(7)
N_TGROUPS = 7
WI_ROWS = 16
HEAD_PAIRS = N_HEADS * HEAD_DIM // LANES
NAT_KA = 0
NAT_KB = HEAD_PAIRS
NAT_F = 2 * HEAD_PAIRS
NAT_KI = 2 * HEAD_PAIRS + 1
N_NAT = (2 * HEAD_PAIRS + 2) * LANES


def _split3(v):
    hi = v.astype(BF16)
    r1 = v - hi.astype(F32)
    mid = r1.astype(BF16)
    r2 = r1 - mid.astype(F32)
    return hi, mid, r2.astype(BF16)


def _dot(a, b):
    return jnp.dot(a, b, preferred_element_type=F32)


def _dot_nt(a, b):
    return lax.dot_general(a, b, (((1,), (1,)), ((), ())), preferred_element_type=F32)


def _dot_tn(a, b):
    return lax.dot_general(a, b, (((0,), (0,)), ((), ())), preferred_element_type=F32)


def _mod_kernel(c_ref, w_ref, b_ref, o_ref):
    c = c_ref[...]
    act = c / (1.0 + jnp.exp(-c))
    a_hi = act.astype(BF16)
    a_lo = (act - a_hi.astype(F32)).astype(BF16)
    w = w_ref[...]
    w_hi = w.astype(BF16)
    w_lo = (w - w_hi.astype(F32)).astype(BF16)
    o_ref[...] = _dot(a_hi, w_hi) + _dot(a_hi, w_lo) + _dot(a_lo, w_hi) + b_ref[...]


def _mod_call(c, w_mod, b_mod):
    b = c.shape[0]
    n = w_mod.shape[1]
    bn = D_MODEL
    return pl.pallas_call(
        _mod_kernel,
        grid=(n // bn,),
        in_specs=[
            pl.BlockSpec((b, D_MODEL), lambda j: (0, 0)),
            pl.BlockSpec((D_MODEL, bn), lambda j: (0, j)),
            pl.BlockSpec((1, bn), lambda j: (0, j)),
        ],
        out_specs=pl.BlockSpec((b, bn), lambda j: (0, j)),
        out_shape=jax.ShapeDtypeStruct((b, n), F32),
        compiler_params=pltpu.CompilerParams(dimension_semantics=("arbitrary",)),
        name="mod",
    )(c, w_mod, b_mod.reshape(1, n))


def _proj_kernel(x_ref, mod_ref, gain_ref, wnat_ref, wt_ref, bf_ref, pos_ref, e_ref,
                 ka_ref, kb_ref, ki_ref, t_ref, wi_ref, carry_ref, *, tm, q_scale):
    si = pl.program_id(1)

    @pl.when(si == 0)
    def _():
        carry_ref[...] = jnp.zeros_like(carry_ref)

    x = x_ref[0]
    ms = jnp.mean(x * x, axis=-1, keepdims=True)
    xn = (x * lax.rsqrt(ms + EPS)) * gain_ref[...]
    shift = mod_ref[0, 0:1, :]
    scale = mod_ref[0, 1:2, :]
    h = (xn * (1.0 + scale) + shift).astype(BF16)

    pn = _dot(h, wnat_ref[...])
    pt = _dot_nt(wt_ref[...], h)

    lane = lax.broadcasted_iota(I32, (tm, LANES), 1)
    fl = pn[:, NAT_F * LANES:(NAT_F + 1) * LANES] + bf_ref[...]
    logf = jnp.minimum(fl, 0.0) - jnp.log(1.0 + jnp.exp(-jnp.abs(fl)))
    logf = jnp.where(lane < N_HEADS, logf, 0.0)
    row = lax.broadcasted_iota(I32, (tm, tm), 0)
    col = lax.broadcasted_iota(I32, (tm, tm), 1)
    tri = jnp.where(col <= row, 1.0, 0.0).astype(BF16)
    l_hi, l_mid, l_lo = _split3(logf)
    cum = carry_ref[...] + (_dot(tri, l_hi) + _dot(tri, l_mid) + _dot(tri, l_lo))
    carry_ref[...] = cum[tm - 1:tm, :]

    f_hi, f_mid, f_lo = _split3(cum * (-LOG2E))
    pieces = (f_hi.astype(F32) + pltpu.roll(f_mid.astype(F32), N_HEADS, axis=1)
              + pltpu.roll(f_lo.astype(F32), 2 * N_HEADS, axis=1))
    faug = _dot(pieces.astype(BF16), e_ref[...])
    pos = pos_ref[...].astype(F32)
    low = lane < HEAD_DIM
    for pair in range(HEAD_PAIRS):
        for grp, out_ref, aug in ((NAT_KA, ka_ref, None), (NAT_KB, kb_ref, pos)):
            both = pn[:, (grp + pair) * LANES:(grp + pair + 1) * LANES]
            for odd in range(2):
                hh = 2 * pair + odd
                k_h = pltpu.roll(both, HEAD_DIM, axis=1) if odd else both
                tail = faug[:, hh * LANES:(hh + 1) * LANES] if aug is None else aug
                out_ref[0, hh] = jnp.where(low, k_h, tail).astype(BF16)
    ki_ref[0] = pn[:, NAT_KI * LANES:(NAT_KI + 1) * LANES].astype(BF16)

    def put(group, val):
        for cc in range(tm // TILE):
            t_ref[0, cc, group * GROUP_W:(group + 1) * GROUP_W, :] = (
                val[:, cc * TILE:(cc + 1) * TILE].astype(BF16))

    def rows(group):
        return pt[group * GROUP_W:(group + 1) * GROUP_W, :]

    def silu(v):
        return v / (1.0 + jnp.exp(-v))

    put(G_QA, rows(G_QA) * q_scale)
    put(G_VA, rows(G_VA))
    put(G_GA, silu(rows(G_GA)))
    put(G_QB, rows(G_QB) * q_scale)
    put(G_VB, rows(G_VB))
    put(G_GB, silu(rows(G_GB)))
    put(G_QI, rows(G_QI))
    wi_ref[0] = pt[N_TGROUPS * GROUP_W:N_TGROUPS * GROUP_W + N_HEADS, :]


def _proj_call(x, mod3, gain, w_nat, w_t, bf_row, pos_aug, e_mats, *, tm):
    b, s, _ = x.shape
    ns = s // tm
    nt = s // TILE
    t_rows = N_TGROUPS * GROUP_W
    kern = functools.partial(
        _proj_kernel, tm=tm,
        q_scale=float(HEAD_DIM ** -0.5 * LOG2E))
    const = dict(pipeline_mode=pl.Buffered(1))
    return pl.pallas_call(
        kern,
        grid=(b, ns),
        in_specs=[
            pl.BlockSpec((1, tm, D_MODEL), lambda i, j: (i, j, 0)),
            pl.BlockSpec((1, 3, D_MODEL), lambda i, j: (i, 0, 0)),
            pl.BlockSpec((1, D_MODEL), lambda i, j: (0, 0)),
            pl.BlockSpec((D_MODEL, N_NAT), lambda i, j: (0, 0), **const),
            pl.BlockSpec((t_rows + WI_ROWS, D_MODEL), lambda i, j: (0, 0), **const),
            pl.BlockSpec((1, LANES), lambda i, j: (0, 0)),
            pl.BlockSpec((tm, LANES), lambda i, j: (j, 0)),
            pl.BlockSpec((LANES, N_HEADS * LANES), lambda i, j: (0, 0), **const),
        ],
        out_specs=[
            pl.BlockSpec((1, N_HEADS, tm, LANES), lambda i, j: (i, 0, j, 0)),
            pl.BlockSpec((1, N_HEADS, tm, LANES), lambda i, j: (i, 0, j, 0)),
            pl.BlockSpec((1, tm, LANES), lambda i, j: (i, j, 0)),
            pl.BlockSpec((1, tm // TILE, t_rows, TILE), lambda i, j: (i, j, 0, 0)),
            pl.BlockSpec((1, N_HEADS, tm), lambda i, j: (i, 0, j)),
        ],
        out_shape=[
            jax.ShapeDtypeStruct((b, N_HEADS, s, LANES), BF16),
            jax.ShapeDtypeStruct((b, N_HEADS, s, LANES), BF16),
            jax.ShapeDtypeStruct((b, s, LANES), BF16),
            jax.ShapeDtypeStruct((b, nt, t_rows, TILE), BF16),
            jax.ShapeDtypeStruct((b, N_HEADS, s), F32),
        ],
        scratch_shapes=[pltpu.VMEM((1, LANES), F32)],
        compiler_params=pltpu.CompilerParams(
            dimension_semantics=("arbitrary", "arbitrary"), vmem_limit_bytes=VMEM_LIMIT),
        name="proj",
    )(x, mod3, gain, w_nat, w_t, bf_row, pos_aug, e_mats)


def _attend(n_last, hb, scores, mask_last, v_tile, s_scr, p_scr, acc_scr, slot0=0):
    t = TILE
    ones_rows = jnp.ones((ACC_ROWS - HEAD_DIM, t), BF16)

    def load_scores(j, ki):
        s = scores(j, ki)
        s_scr[slot0 + j] = s
        return jnp.max(s, axis=0, keepdims=True)

    def pv(j, ki, alpha):
        v_aug = jnp.concatenate([v_tile(j, ki), ones_rows], axis=0)
        acc_scr[slot0 + j] = alpha * acc_scr[slot0 + j] + _dot(v_aug, p_scr[slot0 + j])

    def exp_tile(j, s, m, tile_max):
        m_new = jnp.maximum(m, tile_max)
        p_scr[slot0 + j] = jnp.exp2(s - m_new).astype(BF16)
        return m_new, jnp.exp2(m - m_new)

    init = []
    for j in range(hb):
        tile_max = load_scores(j, 0)
        p_scr[slot0 + j] = jnp.zeros((t, t), BF16)
        acc_scr[slot0 + j] = jnp.zeros((ACC_ROWS, t), F32)
        init.append((jnp.full((1, t), NEG, F32), jnp.ones((1, t), F32), tile_max))

    def body(ki, carry):
        out = []
        for j in range(hb):
            m, alpha, tile_max = carry[j]
            pv(j, jnp.maximum(ki - 1, 0), alpha)
            m, alpha = exp_tile(j, s_scr[slot0 + j], m, tile_max)
            out.append((m, alpha, load_scores(j, ki + 1)))
        return tuple(out)

    carry = lax.fori_loop(0, n_last, body, tuple(init))
    for j in range(hb):
        m, alpha, _ = carry[j]
        pv(j, jnp.maximum(n_last - 1, 0), alpha)
        s = mask_last(s_scr[slot0 + j])
        m, alpha = exp_tile(j, s, m, jnp.max(s, axis=0, keepdims=True))
        pv(j, n_last, alpha)


def _attend_scratch(hb):
    return [pltpu.VMEM((hb, TILE, TILE), F32),
            pltpu.VMEM((hb, TILE, TILE), BF16),
            pltpu.VMEM((hb, ACC_ROWS, TILE), F32)]


def _attend_result(acc_scr, slot, gate):
    acc = acc_scr[slot]
    return ((acc[:HEAD_DIM] / acc[HEAD_DIM:HEAD_DIM + 1]) * gate.astype(F32)).astype(BF16)


def _fox_kernel(q_ref, k_ref, v_ref, g_ref, qaug_ref, o_ref, s_scr, p_scr, acc_scr, *, hb):
    qi = pl.program_id(2)
    t = TILE
    qz = [jnp.concatenate([q_ref[0, 0, j * HEAD_DIM:(j + 1) * HEAD_DIM, :], qaug_ref[...]], axis=0)
          for j in range(hb)]
    causal = lax.broadcasted_iota(I32, (t, t), 0) <= lax.broadcasted_iota(I32, (t, t), 1)

    def scores(j, ki):
        k_t = k_ref[0, j, pl.ds(pl.multiple_of(ki * t, t), t), :]
        return _dot(k_t, qz[j])

    def mask_last(s):
        return jnp.where(causal, s, NEG)

    def v_tile(j, ki):
        return v_ref[0, ki, j * HEAD_DIM:(j + 1) * HEAD_DIM, :]

    _attend(qi, hb, scores, mask_last, v_tile, s_scr, p_scr, acc_scr)
    for j in range(hb):
        rows = slice(j * HEAD_DIM, (j + 1) * HEAD_DIM)
        o_ref[0, 0, rows, :] = _attend_result(acc_scr, j, g_ref[0, 0, rows, :])


def _fox_call(ka, t_arr, qaug, *, hb):
    b, _, s, _ = ka.shape
    nt = s // TILE
    gpb = GROUP_W // (HEAD_DIM * hb)
    rb = HEAD_DIM * hb
    return pl.pallas_call(
        functools.partial(_fox_kernel, hb=hb),
        grid=(b, N_HEADS // hb, nt),
        in_specs=[
            pl.BlockSpec((1, 1, rb, TILE), lambda i, h, q: (i, q, G_QA * gpb + h, 0)),
            pl.BlockSpec((1, hb, s, LANES), lambda i, h, q: (i, h, 0, 0)),
            pl.BlockSpec((1, nt, rb, TILE), lambda i, h, q: (i, 0, G_VA * gpb + h, 0)),
            pl.BlockSpec((1, 1, rb, TILE), lambda i, h, q: (i, q, G_GA * gpb + h, 0)),
            pl.BlockSpec((HEAD_DIM, TILE), lambda i, h, q: (0, 0)),
        ],
        out_specs=pl.BlockSpec((1, 1, rb, TILE), lambda i, h, q: (i, q, h, 0)),
        out_shape=jax.ShapeDtypeStruct((b, nt, GROUP_W, TILE), BF16),
        scratch_shapes=_attend_scratch(hb),
        compiler_params=pltpu.CompilerParams(
            dimension_semantics=("arbitrary", "arbitrary", "arbitrary"),
            vmem_limit_bytes=VMEM_LIMIT),
        name="fox",
    )(t_arr, ka, t_arr, t_arr, qaug)


def _dsa_kernel(qi_ref, wi_ref, ki_ref, q_ref, k_ref, v_ref, g_ref, qaug_ref, o_ref,
                key_ref, hi_ref, lo_ref, bias_ref, thr_ref, lim_ref, s_scr, p_scr, acc_scr,
                *, top_k, idx_bits, idx_scale, hb):
    qidx = pl.program_id(1)
    t = TILE
    nkv = qidx + 1
    srow = lax.broadcasted_iota(I32, (t, t), 0)
    tcol = lax.broadcasted_iota(I32, (t, t), 1)
    zpad = jnp.zeros((LANES - HEAD_DIM, t), BF16)

    qiz = [jnp.concatenate([qi_ref[0, 0, j * HEAD_DIM:(j + 1) * HEAD_DIM, :], zpad], axis=0)
           for j in range(N_HEADS)]
    wrow = [wi_ref[0, j:j + 1, :] for j in range(N_HEADS)]

    def score_keys(ki):
        k_t = ki_ref[0, pl.ds(pl.multiple_of(ki * t, t), t), :]
        sc = jnp.zeros((t, t), F32)
        for j in range(N_HEADS):
            sc = sc + wrow[j] * jnp.maximum(_dot(k_t, qiz[j]), 0.0)
        bits = lax.bitcast_convert_type(sc * idx_scale, I32)
        return bits ^ ((bits >> 31) & 0x7FFFFFFF)

    def put_keys(ki, key):
        key_ref[ki] = key
        hi_ref[ki] = (key >> 16).astype(I16)
        lo_ref[ki] = ((key & 0xFFFF) - 32768).astype(I16)

    def p1(ki, c):
        put_keys(ki, score_keys(ki))
        return c

    lax.fori_loop(0, qidx, p1, 0)
    put_keys(qidx, jnp.where(srow <= tcol, score_keys(qidx), INT_MIN))

    def count(pred):
        def body(ki, c):
            sidx = srow + ki * t
            ind = jnp.where(pred(key_ref[ki], sidx), 1, 0).astype(I32)
            return c + jnp.sum(ind.reshape(t // 8, 8, t), axis=0)
        c8 = lax.fori_loop(0, nkv, body, jnp.zeros((8, t), I32))
        return jnp.sum(c8, axis=0, keepdims=True)

    def count16(ref, pred):
        def body(ki, c):
            ind = jnp.where(pred(ref[ki]), jnp.int16(1), jnp.int16(0))
            parts = [ind[r:r + 16] for r in range(0, t, 16)]
            while len(parts) > 1:
                parts = [a + b for a, b in zip(parts[0::2], parts[1::2])]
            return c + parts[0]
        c16 = lax.fori_loop(0, nkv, body, jnp.zeros((16, t), I16))
        return jnp.sum(c16.astype(I32), axis=0, keepdims=True)

    def kth_largest16(ref, kth):
        c0 = count16(ref, lambda v: v >= jnp.int16(0))
        val0 = jnp.where(c0 >= kth, 0, -32768).astype(I32)

        def bit_body(i, val):
            cand = val + lax.shift_left(jnp.int32(1), 14 - i)
            cand16 = cand.astype(I16)
            c = count16(ref, lambda v: v >= cand16)
            return jnp.where(c >= kth, cand, val)

        return lax.fori_loop(0, 15, bit_body, val0)

    thr_ref[...] = jnp.full((1, t), INT_MIN, I32)
    lim_ref[...] = jnp.zeros((1, t), I32)

    @pl.when(nkv * t > top_k)
    def _():
        thr_hi = kth_largest16(hi_ref, top_k)
        thr_hi16 = thr_hi.astype(I16)
        active = thr_hi > -32768
        above_hi = count16(hi_ref, lambda v: v > thr_hi16)

        def p2(ki, c):
            lo_ref[ki] = jnp.where(hi_ref[ki] == thr_hi16, lo_ref[ki], jnp.int16(-32768))
            return c

        lax.fori_loop(0, nkv, p2, 0)
        thr_lo = kth_largest16(lo_ref, top_k - above_hi)
        thr = jnp.where(active, lax.shift_left(thr_hi, 16) | (thr_lo + 32768), INT_MIN)
        thr_ref[...] = thr

        thr_lo16 = thr_lo.astype(I16)
        c_gt = above_hi + count16(lo_ref, lambda v: v > thr_lo16)
        c_ge = count(lambda kk, sidx: kk >= thr)
        need = top_k - c_gt
        big = jnp.int32(2 ** idx_bits - 1)
        lim_ref[...] = jnp.where(active, big, 0)
        excess = jnp.max(jnp.where(active & (c_ge > top_k), 1, 0))

        @pl.when(excess > 0)
        def _():
            def lim_body(i, lim):
                cand = lim + lax.shift_left(jnp.int32(1), idx_bits - 1 - i)
                c = count(lambda kk, sidx: (kk == thr) & (sidx < cand))
                return jnp.where(c <= need, cand, lim)

            lim = lax.fori_loop(0, idx_bits, lim_body, jnp.zeros((1, t), I32))
            lim_ref[...] = jnp.where(active, lim, 0)

    thr = thr_ref[...]
    lim = lim_ref[...]

    def p3(ki, c):
        kk = key_ref[ki]
        sidx = srow + ki * t
        sel = (kk > thr) | ((kk == thr) & (sidx < lim))
        bias_ref[ki] = jnp.where(sel, 0.0, NEG).astype(F32)
        return c

    lax.fori_loop(0, nkv, p3, 0)

    for j0 in range(0, N_HEADS, hb):
        qz = [jnp.concatenate([q_ref[0, 0, j * HEAD_DIM:(j + 1) * HEAD_DIM, :], qaug_ref[j]], axis=0)
              for j in range(j0, j0 + hb)]

        def scores(n, ki, j0=j0, qz=qz):
            k_t = k_ref[0, j0 + n, pl.ds(pl.multiple_of(ki * t, t), t), :]
            return _dot(k_t, qz[n]) + bias_ref[ki]

        def v_tile(n, ki, j0=j0):
            return v_ref[0, ki, (j0 + n) * HEAD_DIM:(j0 + n + 1) * HEAD_DIM, :]

        _attend(qidx, hb, scores, lambda s: s, v_tile, s_scr, p_scr, acc_scr, slot0=j0)
        for j in range(j0, j0 + hb):
            rows = slice(j * HEAD_DIM, (j + 1) * HEAD_DIM)
            o_ref[0, 0, rows, :] = _attend_result(acc_scr, j, g_ref[0, 0, rows, :])


def _dsa_call(t_arr, wi, ki, kb, qaug, *, hb):
    b, _, s, _ = kb.shape
    nt = s // TILE
    top_k = min(TOPK_MAX, s // 4)
    idx_bits = int(np.ceil(np.log2(s))) + 1
    kern = functools.partial(_dsa_kernel, top_k=top_k, idx_bits=idx_bits, hb=hb,
                             idx_scale=float(HEAD_DIM ** -0.5 * N_HEADS ** -0.5))
    return pl.pallas_call(
        kern,
        grid=(b, nt),
        in_specs=[
            pl.BlockSpec((1, 1, GROUP_W, TILE), lambda i, q: (i, q, G_QI, 0)),
            pl.BlockSpec((1, N_HEADS, TILE), lambda i, q: (i, 0, q)),
            pl.BlockSpec((1, s, LANES), lambda i, q: (i, 0, 0)),
            pl.BlockSpec((1, 1, GROUP_W, TILE), lambda i, q: (i, q, G_QB, 0)),
            pl.BlockSpec((1, N_HEADS, s, LANES), lambda i, q: (i, 0, 0, 0)),
            pl.BlockSpec((1, nt, GROUP_W, TILE), lambda i, q: (i, 0, G_VB, 0)),
            pl.BlockSpec((1, 1, GROUP_W, TILE), lambda i, q: (i, q, G_GB, 0)),
            pl.BlockSpec((N_HEADS, HEAD_DIM, TILE), lambda i, q: (0, 0, 0)),
        ],
        out_specs=pl.BlockSpec((1, 1, GROUP_W, TILE), lambda i, q: (i, q, 0, 0)),
        out_shape=jax.ShapeDtypeStruct((b, nt, GROUP_W, TILE), BF16),
        scratch_shapes=[
            pltpu.VMEM((nt, TILE, TILE), I32),
            pltpu.VMEM((nt, TILE, TILE), I16),
            pltpu.VMEM((nt, TILE, TILE), I16),
            pltpu.VMEM((nt, TILE, TILE), F32),
            pltpu.VMEM((1, TILE), I32),
            pltpu.VMEM((1, TILE), I32),
        ] + _attend_scratch(N_HEADS),
        compiler_params=pltpu.CompilerParams(
            dimension_semantics=("arbitrary", "arbitrary"), vmem_limit_bytes=VMEM_LIMIT),
        name="dsa",
    )(t_arr, wi, ki, t_arr, kb, t_arr, t_arr, qaug)


def _out_kernel(x_ref, ya_ref, yb_ref, w_ref, gate_ref, gain_ref, o_ref, *, nsub):
    for cc in range(nsub):
        rows = slice(cc * TILE, (cc + 1) * TILE)
        y_t = jnp.concatenate([ya_ref[0, cc], yb_ref[0, cc]], axis=0)
        y = _dot_tn(y_t, w_ref[...])
        z = x_ref[0, rows, :] + gate_ref[0] * y
        ms = jnp.mean(z * z, axis=-1, keepdims=True)
        o_ref[0, rows, :] = (z * lax.rsqrt(ms + EPS)) * gain_ref[...]


def _out_call(x, ya, yb, w_out, gate3, final_gain, *, tm):
    b, s, _ = x.shape
    nsub = tm // TILE
    return pl.pallas_call(
        functools.partial(_out_kernel, nsub=nsub),
        grid=(b, s // tm),
        in_specs=[
            pl.BlockSpec((1, tm, D_MODEL), lambda i, q: (i, q, 0)),
            pl.BlockSpec((1, nsub, GROUP_W, TILE), lambda i, q: (i, q, 0, 0)),
            pl.BlockSpec((1, nsub, GROUP_W, TILE), lambda i, q: (i, q, 0, 0)),
            pl.BlockSpec((2 * GROUP_W, D_MODEL), lambda i, q: (0, 0)),
            pl.BlockSpec((1, 1, D_MODEL), lambda i, q: (i, 0, 0)),
            pl.BlockSpec((1, D_MODEL), lambda i, q: (0, 0)),
        ],
        out_specs=pl.BlockSpec((1, tm, D_MODEL), lambda i, q: (i, q, 0)),
        out_shape=jax.ShapeDtypeStruct((b, s, D_MODEL), F32),
        compiler_params=pltpu.CompilerParams(
            dimension_semantics=("arbitrary", "arbitrary"), vmem_limit_bytes=VMEM_LIMIT),
        name="outproj",
    )(x, ya, yb, w_out, gate3, final_gain)


def _layout_weights(w_in):
    widths = (GROUP_W, GROUP_W, GROUP_W, N_HEADS, GROUP_W, GROUP_W, GROUP_W, GROUP_W,
              N_HEADS * HEAD_DIM, HEAD_DIM, N_HEADS, GROUP_W)
    offs = np.concatenate([[0], np.cumsum(widths)])
    (w_qa, w_ka, w_va, w_f, w_ga, w_qb, w_kb, w_vb, w_qi, w_ki, w_wi, w_gb) = (
        w_in[:, offs[n]:offs[n + 1]] for n in range(len(widths)))

    def pad_cols(w):
        return jnp.pad(w, ((0, 0), (0, LANES - w.shape[1])))

    w_nat = jnp.concatenate([w_ka, w_kb, pad_cols(w_f), pad_cols(w_ki)], axis=1)
    w_t = jnp.concatenate([w_qa, w_va, w_ga, w_qb, w_vb, w_gb, w_qi,
                           jnp.pad(w_wi, ((0, 0), (0, WI_ROWS - N_HEADS)))], axis=1).T
    return w_nat.astype(BF16), w_t.astype(BF16)


def _constants(s):
    e = np.zeros((LANES, N_HEADS * LANES), np.float32)
    for i in range(3):
        for h in range(N_HEADS):
            e[N_HEADS * i + h, h * LANES + HEAD_DIM + i] = 1.0
    pos = np.zeros((s, LANES), np.float32)
    sidx = np.arange(s)
    for i in range(3):
        pos[:, HEAD_DIM + i] = sidx // 64
        pos[:, HEAD_DIM + 3 + i] = sidx % 64
    qa = np.zeros((HEAD_DIM, TILE), np.float32)
    qa[0:3, :] = 1.0
    return jnp.asarray(e, BF16), jnp.asarray(pos, BF16), jnp.asarray(qa, BF16)


def _alibi_rows():
    i = jnp.arange(1, N_HEADS + 1, dtype=F32)
    slopes = jnp.exp2(-8.0 * i / N_HEADS) * LOG2E
    c64 = _split3(slopes * 64.0)
    c1 = _split3(slopes)
    rows = jnp.stack(list(c64) + list(c1), axis=1)
    rows = jnp.pad(rows, ((0, 0), (0, HEAD_DIM - 6)))
    return jnp.broadcast_to(rows[:, :, None], (N_HEADS, HEAD_DIM, TILE)).astype(BF16)


def kernel(x, c, w_mod, b_mod, norm_gain, w_in, b_forget, w_out, final_gain):
    b, s, d = x.shape
    assert d == D_MODEL and s % TILE == 0
    assert w_mod.shape[0] == 1, "single-layer block"
    tm = ROW_BLOCK if s % ROW_BLOCK == 0 else TILE

    mod = _mod_call(c, w_mod[0], b_mod[0])
    mod3 = mod.reshape(b, 3, D_MODEL)
    w_nat, w_t = _layout_weights(w_in[0])
    e_mats, pos_aug, qaug_a = _constants(s)
    bf_row = jnp.pad(b_forget[0], (0, LANES - N_HEADS)).reshape(1, LANES)

    ka, kb, ki, t_arr, wi = _proj_call(
        x, mod3, norm_gain[0].reshape(1, D_MODEL), w_nat, w_t, bf_row, pos_aug, e_mats, tm=tm)

    ya = _fox_call(ka, t_arr, qaug_a, hb=HEADS_PER_BODY)
    yb = _dsa_call(t_arr, wi, ki, kb, _alibi_rows(), hb=HEADS_PER_BODY)

    gate3 = mod3[:, 2:3, :]
    return _out_call(x, ya, yb, w_out[0].astype(BF16), gate3, final_gain.reshape(1, D_MODEL), tm=tm)
```

```python
import functools

import jax
import jax.numpy as jnp
import numpy as np
from jax import lax
from jax.experimental import pallas as pl
from jax.experimental.pallas import tpu as pltpu

F32 = jnp.float32
BF16 = jnp.bfloat16
I32 = jnp.int32

D_MODEL = 1024
HEAD_DIM = 64
N_HEADS = 8
GROUP_W = N_HEADS * HEAD_DIM
TOPK_MAX = 256
EPS = 1e-6
LOG2E = 1.4426950408889634
NEG = -1e30

LANES = 128
TILE = 256
ROW_BLOCK = 512
HEADS_PER_BODY = 8
ACC_ROWS = HEAD_DIM + 16
VMEM_LIMIT = 56 * 1024 * 1024

G_QA, G_VA, G_GA, G_QB, G_VB, G_GB, G_QI = range(7)
N_TGROUPS = 7
WI_ROWS = 16
HEAD_PAIRS = N_HEADS * HEAD_DIM // LANES
NAT_KA = 0
NAT_KB = HEAD_PAIRS
NAT_F = 2 * HEAD_PAIRS
NAT_KI = 2 * HEAD_PAIRS + 1
N_NAT = (2 * HEAD_PAIRS + 2) * LANES


def _split3(v):
    hi = v.astype(BF16)
    r1 = v - hi.astype(F32)
    mid = r1.astype(BF16)
    r2 = r1 - mid.astype(F32)
    return hi, mid, r2.astype(BF16)


def _dot(a, b):
    return jnp.dot(a, b, preferred_element_type=F32)


def _dot_nt(a, b):
    return lax.dot_general(a, b, (((1,), (1,)), ((), ())), preferred_element_type=F32)


def _dot_tn(a, b):
    return lax.dot_general(a, b, (((0,), (0,)), ((), ())), preferred_element_type=F32)


def _mod_kernel(c_ref, w_ref, b_ref, o_ref):
    c = c_ref[...]
    act = c / (1.0 + jnp.exp(-c))
    a_hi = act.astype(BF16)
    a_lo = (act - a_hi.astype(F32)).astype(BF16)
    w = w_ref[...]
    w_hi = w.astype(BF16)
    w_lo = (w - w_hi.astype(F32)).astype(BF16)
    o_ref[...] = _dot(a_hi, w_hi) + _dot(a_hi, w_lo) + _dot(a_lo, w_hi) + b_ref[...]


def _mod_call(c, w_mod, b_mod):
    b = c.shape[0]
    n = w_mod.shape[1]
    bn = D_MODEL
    return pl.pallas_call(
        _mod_kernel,
        grid=(n // bn,),
        in_specs=[
            pl.BlockSpec((b, D_MODEL), lambda j: (0, 0)),
            pl.BlockSpec((D_MODEL, bn), lambda j: (0, j)),
            pl.BlockSpec((1, bn), lambda j: (0, j)),
        ],
        out_specs=pl.BlockSpec((b, bn), lambda j: (0, j)),
        out_shape=jax.ShapeDtypeStruct((b, n), F32),
        compiler_params=pltpu.CompilerParams(dimension_semantics=("arbitrary",)),
        name="mod",
    )(c, w_mod, b_mod.reshape(1, n))


def _proj_kernel(x_ref, mod_ref, gain_ref, wnat_ref, wt_ref, bf_ref, pos_ref, e_ref,
                 ka_ref, kb_ref, ki_ref, t_ref, wi_ref, carry_ref, *, tm, q_scale):
    si = pl.program_id(1)

    @pl.when(si == 0)
    def _():
        carry_ref[...] = jnp.zeros_like(carry_ref)

    x = x_ref[0]
    ms = jnp.mean(x * x, axis=-1, keepdims=True)
    xn = (x * lax.rsqrt(ms + EPS)) * gain_ref[...]
    shift = mod_ref[0, 0:1, :]
    scale = mod_ref[0, 1:2, :]
    h = (xn * (1.0 + scale) + shift).astype(BF16)

    pn = _dot(h, wnat_ref[...])
    pt = _dot_nt(wt_ref[...], h)

    lane = lax.broadcasted_iota(I32, (tm, LANES), 1)
    fl = pn[:, NAT_F * LANES:(NAT_F + 1) * LANES] + bf_ref[...]
    logf = jnp.minimum(fl, 0.0) - jnp.log(1.0 + jnp.exp(-jnp.abs(fl)))
    logf = jnp.where(lane < N_HEADS, logf, 0.0)
    row = lax.broadcasted_iota(I32, (tm, tm), 0)
    col = lax.broadcasted_iota(I32, (tm, tm), 1)
    tri = jnp.where(col <= row, 1.0, 0.0).astype(BF16)
    l_hi, l_mid, l_lo = _split3(logf)
    cum = carry_ref[...] + (_dot(tri, l_hi) + _dot(tri, l_mid) + _dot(tri, l_lo))
    carry_ref[...] = cum[tm - 1:tm, :]

    f_hi, f_mid, f_lo = _split3(cum * (-LOG2E))
    pieces = (f_hi.astype(F32) + pltpu.roll(f_mid.astype(F32), N_HEADS, axis=1)
              + pltpu.roll(f_lo.astype(F32), 2 * N_HEADS, axis=1))
    faug = _dot(pieces.astype(BF16), e_ref[...])
    pos = pos_ref[...].astype(F32)
    low = lane < HEAD_DIM
    for pair in range(HEAD_PAIRS):
        for grp, out_ref, aug in ((NAT_KA, ka_ref, None), (NAT_KB, kb_ref, pos)):
            both = pn[:, (grp + pair) * LANES:(grp + pair + 1) * LANES]
            for odd in range(2):
                hh = 2 * pair + odd
                k_h = pltpu.roll(both, HEAD_DIM, axis=1) if odd else both
                tail = faug[:, hh * LANES:(hh + 1) * LANES] if aug is None else aug
                out_ref[0, hh] = jnp.where(low, k_h, tail).astype(BF16)
    ki_ref[0] = pn[:, NAT_KI * LANES:(NAT_KI + 1) * LANES].astype(BF16)

    def put(group, val):
        for cc in range(tm // TILE):
            t_ref[0, cc, group * GROUP_W:(group + 1) * GROUP_W, :] = (
                val[:, cc * TILE:(cc + 1) * TILE].astype(BF16))

    def rows(group):
        return pt[group * GROUP_W:(group + 1) * GROUP_W, :]

    def silu(v):
        return v / (1.0 + jnp.exp(-v))

    put(G_QA, rows(G_QA) * q_scale)
    put(G_VA, rows(G_VA))
    put(G_GA, silu(rows(G_GA)))
    put(G_QB, rows(G_QB) * q_scale)
    put(G_VB, rows(G_VB))
    put(G_GB, silu(rows(G_GB)))
    put(G_QI, rows(G_QI))
    wi_ref[0] = pt[N_TGROUPS * GROUP_W:N_TGROUPS * GROUP_W + N_HEADS, :]


def _proj_call(x, mod3, gain, w_nat, w_t, bf_row, pos_aug, e_mats, *, tm):
    b, s, _ = x.shape
    ns = s // tm
    nt = s // TILE
    t_rows = N_TGROUPS * GROUP_W
    kern = functools.partial(
        _proj_kernel, tm=tm,
        q_scale=float(HEAD_DIM ** -0.5 * LOG2E))
    const = dict(pipeline_mode=pl.Buffered(1))
    return pl.pallas_call(
        kern,
        grid=(b, ns),
        in_specs=[
            pl.BlockSpec((1, tm, D_MODEL), lambda i, j: (i, j, 0)),
            pl.BlockSpec((1, 3, D_MODEL), lambda i, j: (i, 0, 0)),
            pl.BlockSpec((1, D_MODEL), lambda i, j: (0, 0)),
            pl.BlockSpec((D_MODEL, N_NAT), lambda i, j: (0, 0), **const),
            pl.BlockSpec((t_rows + WI_ROWS, D_MODEL), lambda i, j: (0, 0), **const),
            pl.BlockSpec((1, LANES), lambda i, j: (0, 0)),
            pl.BlockSpec((tm, LANES), lambda i, j: (j, 0)),
            pl.BlockSpec((LANES, N_HEADS * LANES), lambda i, j: (0, 0), **const),
        ],
        out_specs=[
            pl.BlockSpec((1, N_HEADS, tm, LANES), lambda i, j: (i, 0, j, 0)),
            pl.BlockSpec((1, N_HEADS, tm, LANES), lambda i, j: (i, 0, j, 0)),
            pl.BlockSpec((1, tm, LANES), lambda i, j: (i, j, 0)),
            pl.BlockSpec((1, tm // TILE, t_rows, TILE), lambda i, j: (i, j, 0, 0)),
            pl.BlockSpec((1, N_HEADS, tm), lambda i, j: (i, 0, j)),
        ],
        out_shape=[
            jax.ShapeDtypeStruct((b, N_HEADS, s, LANES), BF16),
            jax.ShapeDtypeStruct((b, N_HEADS, s, LANES), BF16),
            jax.ShapeDtypeStruct((b, s, LANES), BF16),
            jax.ShapeDtypeStruct((b, nt, t_rows, TILE), BF16),
            jax.ShapeDtypeStruct((b, N_HEADS, s), F32),
        ],
        scratch_shapes=[pltpu.VMEM((1, LANES), F32)],
        compiler_params=pltpu.CompilerParams(
            dimension_semantics=("arbitrary", "arbitrary"), vmem_limit_bytes=VMEM_LIMIT),
        name="proj",
    )(x, mod3, gain, w_nat, w_t, bf_row, pos_aug, e_mats)


def _tile_rows(ki):
    if isinstance(ki, int):
        return pl.ds(ki * TILE, TILE)
    return pl.ds(pl.multiple_of(ki * TILE, TILE), TILE)


def _prev_tile(ki):
    return max(ki - 1, 0) if isinstance(ki, int) else jnp.maximum(ki - 1, 0)


def _attend(n_last, hb, scores, mask_last, v_tile):
    t = TILE
    ones_rows = jnp.ones((ACC_ROWS - HEAD_DIM, t), BF16)

    def pv(j, ki, alpha, acc, p):
        v_aug = jnp.concatenate([v_tile(j, ki), ones_rows], axis=0)
        upd = _dot(v_aug, p)
        return upd if acc is None else alpha * acc + upd

    def exp_tile(s, m, tile_max):
        m_new = jnp.maximum(m, tile_max)
        return m_new, jnp.exp2(m - m_new), jnp.exp2(s - m_new).astype(BF16)

    s = [scores(j, 0) for j in range(hb)]
    m = [jnp.full((1, t), NEG, F32) for _ in range(hb)]
    acc = [None] * hb
    for ki in range(n_last):
        for j in range(hb):
            m[j], alpha, p = exp_tile(s[j], m[j], jnp.max(s[j], axis=0, keepdims=True))
            s[j] = scores(j, ki + 1)
            acc[j] = pv(j, ki, alpha, acc[j], p)
    for j in range(hb):
        last = mask_last(s[j])
        m[j], alpha, p = exp_tile(last, m[j], jnp.max(last, axis=0, keepdims=True))
        acc[j] = pv(j, n_last, alpha, acc[j], p)
    return acc


def _attend_result(acc, gate):
    return ((acc[:HEAD_DIM] / acc[HEAD_DIM:HEAD_DIM + 1]) * gate.astype(F32)).astype(BF16)


def _fox_kernel(q_ref, k_ref, v_ref, g_ref, qaug_ref, o_ref, *, hb):
    qi = pl.program_id(2)
    t = TILE
    qz = [jnp.concatenate([q_ref[0, 0, j * HEAD_DIM:(j + 1) * HEAD_DIM, :], qaug_ref[...]], axis=0)
          for j in range(hb)]
    causal = lax.broadcasted_iota(I32, (t, t), 0) <= lax.broadcasted_iota(I32, (t, t), 1)

    def scores(j, ki):
        k_t = k_ref[0, j, _tile_rows(ki), :]
        return _dot(k_t, qz[j])

    def mask_last(s):
        return jnp.where(causal, s, NEG)

    def v_tile(j, ki):
        return v_ref[0, ki, j * HEAD_DIM:(j + 1) * HEAD_DIM, :]

    def run(n_last):
        acc = _attend(n_last, hb, scores, mask_last, v_tile)
        for j in range(hb):
            rows = slice(j * HEAD_DIM, (j + 1) * HEAD_DIM)
            o_ref[0, 0, rows, :] = _attend_result(acc[j], g_ref[0, 0, rows, :])

    for qv in range(v_ref.shape[1]):
        pl.when(qi == qv)(functools.partial(run, qv))


def _fox_call(ka, t_arr, qaug, *, hb):
    b, _, s, _ = ka.shape
    nt = s // TILE
    gpb = GROUP_W // (HEAD_DIM * hb)
    rb = HEAD_DIM * hb
    return pl.pallas_call(
        functools.partial(_fox_kernel, hb=hb),
        grid=(b, N_HEADS // hb, nt),
        in_specs=[
            pl.BlockSpec((1, 1, rb, TILE), lambda i, h, q: (i, q, G_QA * gpb + h, 0)),
            pl.BlockSpec((1, hb, s, LANES), lambda i, h, q: (i, h, 0, 0)),
            pl.BlockSpec((1, nt, rb, TILE), lambda i, h, q: (i, 0, G_VA * gpb + h, 0)),
            pl.BlockSpec((1, 1, rb, TILE), lambda i, h, q: (i, q, G_GA * gpb + h, 0)),
            pl.BlockSpec((HEAD_DIM, TILE), lambda i, h, q: (0, 0)),
        ],
        out_specs=pl.BlockSpec((1, 1, rb, TILE), lambda i, h, q: (i, q, h, 0)),
        out_shape=jax.ShapeDtypeStruct((b, nt, GROUP_W, TILE), BF16),
        compiler_params=pltpu.CompilerParams(
            dimension_semantics=("arbitrary", "arbitrary", "arbitrary"),
            vmem_limit_bytes=VMEM_LIMIT),
        name="fox",
    )(t_arr, ka, t_arr, t_arr, qaug)


def _dsa_kernel(qi_ref, wi_ref, ki_ref, q_ref, k_ref, v_ref, g_ref, qaug_ref, o_ref,
                sc_ref, sb_ref, bias_ref, thr_ref, lim_ref,
                *, top_k, idx_bits, idx_scale, hb):
    qidx = pl.program_id(1)
    t = TILE
    nkv = qidx + 1
    srow = lax.broadcasted_iota(I32, (t, t), 0)
    tcol = lax.broadcasted_iota(I32, (t, t), 1)
    zpad = jnp.zeros((LANES - HEAD_DIM, t), BF16)

    qiz = [jnp.concatenate([qi_ref[0, 0, j * HEAD_DIM:(j + 1) * HEAD_DIM, :], zpad], axis=0)
           for j in range(N_HEADS)]
    wrow = [wi_ref[0, j:j + 1, :] for j in range(N_HEADS)]

    def index_scores(ki):
        k_t = ki_ref[0, _tile_rows(ki), :]
        sc = jnp.zeros((t, t), F32)
        for j in range(N_HEADS):
            sc = sc + wrow[j] * jnp.maximum(_dot(k_t, qiz[j]), 0.0)
        return sc * idx_scale

    def put_scores(ki, sc):
        sc_ref[ki] = sc
        sb_ref[ki] = sc.astype(BF16)

    def index_tiles(n_last):
        for ki in range(n_last):
            put_scores(ki, index_scores(ki))
        put_scores(n_last, jnp.where(srow <= tcol, index_scores(n_last), -jnp.inf))

    for qv in range(v_ref.shape[1]):
        pl.when(qidx == qv)(functools.partial(index_tiles, qv))

    def count(pred):
        def body(ki, c):
            sidx = srow + ki * t
            ind = jnp.where(pred(sc_ref[ki], sidx), 1, 0).astype(I32)
            return c + jnp.sum(ind.reshape(t // 8, 8, t), axis=0)
        c8 = lax.fori_loop(0, nkv, body, jnp.zeros((8, t), I32))
        return jnp.sum(c8, axis=0, keepdims=True)

    def count_rounded(cand):
        def body(ki, c):
            ind = jnp.where(sb_ref[ki] >= cand, jnp.ones((), BF16), jnp.zeros((), BF16))
            parts = [ind[r:r + 16] for r in range(0, t, 16)]
            while len(parts) > 1:
                parts = [x + y for x, y in zip(parts[0::2], parts[1::2])]
            return c + parts[0]
        c16 = lax.fori_loop(0, nkv, body, jnp.zeros((16, t), BF16))
        return jnp.sum(c16.astype(F32), axis=0, keepdims=True).astype(I32)

    def key_to_f32(key):
        return lax.bitcast_convert_type(key ^ ((key >> 31) & 0x7FFFFFFF), F32)

    thr_ref[...] = jnp.full((1, t), -jnp.inf, F32)
    lim_ref[...] = jnp.zeros((1, t), I32)

    @pl.when(nkv * t > top_k)
    def _():
        def coarse_value(key16):
            return key_to_f32(lax.shift_left(key16, 16)).astype(BF16)

        c0 = count_rounded(coarse_value(jnp.zeros((1, t), I32)))
        key0 = jnp.where(c0 >= top_k, 0, -32768).astype(I32)

        def coarse_body(i, key16):
            cand = key16 + lax.shift_left(jnp.int32(1), 14 - i)
            return jnp.where(count_rounded(coarse_value(cand)) >= top_k, cand, key16)

        key16 = lax.fori_loop(0, 15, coarse_body, key0)
        active = key16 > -32768

        base = lax.shift_left(key16, 16) - 65536

        def fine_body(i, off):
            cand = off + lax.shift_left(jnp.int32(1), 16 - i)
            thr_c = key_to_f32(base + cand)
            c = count(lambda sc, sidx: sc >= thr_c)
            return jnp.where(c >= top_k, cand, off)

        off = lax.fori_loop(0, 17, fine_body, jnp.zeros((1, t), I32))
        thr = jnp.where(active, key_to_f32(base + off), -jnp.inf)
        thr_ref[...] = thr

        c_gt = count(lambda sc, sidx: sc > thr)
        c_ge = count(lambda sc, sidx: sc >= thr)
        need = top_k - c_gt
        big = jnp.int32(2 ** idx_bits - 1)
        lim_ref[...] = jnp.where(active, big, 0)
        excess = jnp.max(jnp.where(active & (c_ge > top_k), 1, 0))

        @pl.when(excess > 0)
        def _():
            def lim_body(i, lim):
                cand = lim + lax.shift_left(jnp.int32(1), idx_bits - 1 - i)
                c = count(lambda sc, sidx: (sc == thr) & (sidx < cand))
                return jnp.where(c <= need, cand, lim)

            lim = lax.fori_loop(0, idx_bits, lim_body, jnp.zeros((1, t), I32))
            lim_ref[...] = jnp.where(active, lim, 0)

    thr = thr_ref[...]
    lim = lim_ref[...]

    def p3(ki, c):
        sc = sc_ref[ki]
        sidx = srow + ki * t
        sel = (sc > thr) | ((sc == thr) & (sidx < lim))
        bias_ref[ki] = jnp.where(sel, 0.0, NEG).astype(F32)
        return c

    lax.fori_loop(0, nkv, p3, 0)

    def attend_heads(n_last):
        for j0 in range(0, N_HEADS, hb):
            qz = [jnp.concatenate([q_ref[0, 0, j * HEAD_DIM:(j + 1) * HEAD_DIM, :], qaug_ref[j]],
                                  axis=0) for j in range(j0, j0 + hb)]

            def scores(n, ki, j0=j0, qz=qz):
                k_t = k_ref[0, j0 + n, _tile_rows(ki), :]
                return _dot(k_t, qz[n]) + bias_ref[ki]

            def v_tile(n, ki, j0=j0):
                return v_ref[0, ki, (j0 + n) * HEAD_DIM:(j0 + n + 1) * HEAD_DIM, :]

            acc = _attend(n_last, hb, scores, lambda s: s, v_tile)
            for n, j in enumerate(range(j0, j0 + hb)):
                rows = slice(j * HEAD_DIM, (j + 1) * HEAD_DIM)
                o_ref[0, 0, rows, :] = _attend_result(acc[n], g_ref[0, 0, rows, :])

    for qv in range(v_ref.shape[1]):
        pl.when(qidx == qv)(functools.partial(attend_heads, qv))


def _dsa_call(t_arr, wi, ki, kb, qaug, *, hb):
    b, _, s, _ = kb.shape
    nt = s // TILE
    top_k = min(TOPK_MAX, s // 4)
    idx_bits = int(np.ceil(np.log2(s))) + 1
    kern = functools.partial(_dsa_kernel, top_k=top_k, idx_bits=idx_bits, hb=hb,
                             idx_scale=float(HEAD_DIM ** -0.5 * N_HEADS ** -0.5))
    return pl.pallas_call(
        kern,
        grid=(b, nt),
        in_specs=[
            pl.BlockSpec((1, 1, GROUP_W, TILE), lambda i, q: (i, q, G_QI, 0)),
            pl.BlockSpec((1, N_HEADS, TILE), lambda i, q: (i, 0, q)),
            pl.BlockSpec((1, s, LANES), lambda i, q: (i, 0, 0)),
            pl.BlockSpec((1, 1, GROUP_W, TILE), lambda i, q: (i, q, G_QB, 0)),
            pl.BlockSpec((1, N_HEADS, s, LANES), lambda i, q: (i, 0, 0, 0)),
            pl.BlockSpec((1, nt, GROUP_W, TILE), lambda i, q: (i, 0, G_VB, 0)),
            pl.BlockSpec((1, 1, GROUP_W, TILE), lambda i, q: (i, q, G_GB, 0)),
            pl.BlockSpec((N_HEADS, HEAD_DIM, TILE), lambda i, q: (0, 0, 0)),
        ],
        out_specs=pl.BlockSpec((1, 1, GROUP_W, TILE), lambda i, q: (i, q, 0, 0)),
        out_shape=jax.ShapeDtypeStruct((b, nt, GROUP_W, TILE), BF16),
        scratch_shapes=[
            pltpu.VMEM((nt, TILE, TILE), F32),
            pltpu.VMEM((nt, TILE, TILE), BF16),
            pltpu.VMEM((nt, TILE, TILE), F32),
            pltpu.VMEM((1, TILE), F32),
            pltpu.VMEM((1, TILE), I32),
        ],
        compiler_params=pltpu.CompilerParams(
            dimension_semantics=("arbitrary", "arbitrary"), vmem_limit_bytes=VMEM_LIMIT),
        name="dsa",
    )(t_arr, wi, ki, t_arr, kb, t_arr, t_arr, qaug)


def _out_kernel(x_ref, ya_ref, yb_ref, w_ref, gate_ref, gain_ref, o_ref, *, nsub):
    for cc in range(nsub):
        rows = slice(cc * TILE, (cc + 1) * TILE)
        y_t = jnp.concatenate([ya_ref[0, cc], yb_ref[0, cc]], axis=0)
        y = _dot_tn(y_t, w_ref[...])
        z = x_ref[0, rows, :] + gate_ref[0] * y
        ms = jnp.mean(z * z, axis=-1, keepdims=True)
        o_ref[0, rows, :] = (z * lax.rsqrt(ms + EPS)) * gain_ref[...]


def _out_call(x, ya, yb, w_out, gate3, final_gain, *, tm):
    b, s, _ = x.shape
    nsub = tm // TILE
    return pl.pallas_call(
        functools.partial(_out_kernel, nsub=nsub),
        grid=(b, s // tm),
        in_specs=[
            pl.BlockSpec((1, tm, D_MODEL), lambda i, q: (i, q, 0)),
            pl.BlockSpec((1, nsub, GROUP_W, TILE), lambda i, q: (i, q, 0, 0)),
            pl.BlockSpec((1, nsub, GROUP_W, TILE), lambda i, q: (i, q, 0, 0)),
            pl.BlockSpec((2 * GROUP_W, D_MODEL), lambda i, q: (0, 0)),
            pl.BlockSpec((1, 1, D_MODEL), lambda i, q: (i, 0, 0)),
            pl.BlockSpec((1, D_MODEL), lambda i, q: (0, 0)),
        ],
        out_specs=pl.BlockSpec((1, tm, D_MODEL), lambda i, q: (i, q, 0)),
        out_shape=jax.ShapeDtypeStruct((b, s, D_MODEL), F32),
        compiler_params=pltpu.CompilerParams(
            dimension_semantics=("arbitrary", "arbitrary"), vmem_limit_bytes=VMEM_LIMIT),
        name="outproj",
    )(x, ya, yb, w_out, gate3, final_gain)


def _layout_weights(w_in):
    widths = (GROUP_W, GROUP_W, GROUP_W, N_HEADS, GROUP_W, GROUP_W, GROUP_W, GROUP_W,
              N_HEADS * HEAD_DIM, HEAD_DIM, N_HEADS, GROUP_W)
    offs = np.concatenate([[0], np.cumsum(widths)])
    (w_qa, w_ka, w_va, w_f, w_ga, w_qb, w_kb, w_vb, w_qi, w_ki, w_wi, w_gb) = (
        w_in[:, offs[n]:offs[n + 1]] for n in range(len(widths)))

    def pad_cols(w):
        return jnp.pad(w, ((0, 0), (0, LANES - w.shape[1])))

    w_nat = jnp.concatenate([w_ka, w_kb, pad_cols(w_f), pad_cols(w_ki)], axis=1)
    w_t = jnp.concatenate([w_qa, w_va, w_ga, w_qb, w_vb, w_gb, w_qi,
                           jnp.pad(w_wi, ((0, 0), (0, WI_ROWS - N_HEADS)))], axis=1).T
    return w_nat.astype(BF16), w_t.astype(BF16)


def _constants(s):
    e = np.zeros((LANES, N_HEADS * LANES), np.float32)
    for i in range(3):
        for h in range(N_HEADS):
            e[N_HEADS * i + h, h * LANES + HEAD_DIM + i] = 1.0
    pos = np.zeros((s, LANES), np.float32)
    sidx = np.arange(s)
    for i in range(3):
        pos[:, HEAD_DIM + i] = sidx // 64
        pos[:, HEAD_DIM + 3 + i] = sidx % 64
    qa = np.zeros((HEAD_DIM, TILE), np.float32)
    qa[0:3, :] = 1.0
    return jnp.asarray(e, BF16), jnp.asarray(pos, BF16), jnp.asarray(qa, BF16)


def _alibi_rows():
    i = jnp.arange(1, N_HEADS + 1, dtype=F32)
    slopes = jnp.exp2(-8.0 * i / N_HEADS) * LOG2E
    c64 = _split3(slopes * 64.0)
    c1 = _split3(slopes)
    rows = jnp.stack(list(c64) + list(c1), axis=1)
    rows = jnp.pad(rows, ((0, 0), (0, HEAD_DIM - 6)))
    return jnp.broadcast_to(rows[:, :, None], (N_HEADS, HEAD_DIM, TILE)).astype(BF16)


def kernel(x, c, w_mod, b_mod, norm_gain, w_in, b_forget, w_out, final_gain):
    b, s, d = x.shape
    assert d == D_MODEL and s % TILE == 0
    assert w_mod.shape[0] == 1, "single-layer block"
    tm = ROW_BLOCK if s % ROW_BLOCK == 0 else TILE

    mod = _mod_call(c, w_mod[0], b_mod[0])
    mod3 = mod.reshape(b, 3, D_MODEL)
    w_nat, w_t = _layout_weights(w_in[0])
    e_mats, pos_aug, qaug_a = _constants(s)
    bf_row = jnp.pad(b_forget[0], (0, LANES - N_HEADS)).reshape(1, LANES)

    ka, kb, ki, t_arr, wi = _proj_call(
        x, mod3, norm_gain[0].reshape(1, D_MODEL), w_nat, w_t, bf_row, pos_aug, e_mats, tm=tm)

    ya = _fox_call(ka, t_arr, qaug_a, hb=HEADS_PER_BODY)
    yb = _dsa_call(t_arr, wi, ki, kb, _alibi_rows(), hb=HEADS_PER_BODY)

    gate3 = mod3[:, 2:3, :]
    return _out_call(x, ya, yb, w_out[0].astype(BF16), gate3, final_gain.reshape(1, D_MODEL), tm=tm)
```

```python
import functools

import jax
import jax.numpy as jnp
import numpy as np
from jax import lax
from jax.experimental import pallas as pl
from jax.experimental.pallas import tpu as pltpu

F32 = jnp.float32
BF16 = jnp.bfloat16
I32 = jnp.int32
I16 = jnp.int16

D_MODEL = 1024
HEAD_DIM = 64
N_HEADS = 8
GROUP_W = N_HEADS * HEAD_DIM
TOPK_MAX = 256
EPS = 1e-6
LOG2E = 1.4426950408889634
NEG = -1e30

LANES = 128
TILE = 256
ROW_BLOCK = 512
HEADS_PER_BODY = 8
ACC_ROWS = HEAD_DIM + 16
VMEM_LIMIT = 56 * 1024 * 1024

G_QA, G_VA, G_GA, G_QB, G_VB, G_GB, G_QI = range(7)
N_TGROUPS = 7
WI_ROWS = 16
HEAD_PAIRS = N_HEADS * HEAD_DIM // LANES
NAT_KA = 0
NAT_KB = HEAD_PAIRS
NAT_F = 2 * HEAD_PAIRS
NAT_KI = 2 * HEAD_PAIRS + 1
N_NAT = (2 * HEAD_PAIRS + 2) * LANES


def _split3(v):
    hi = v.astype(BF16)
    r1 = v - hi.astype(F32)
    mid = r1.astype(BF16)
    r2 = r1 - mid.astype(F32)
    return hi, mid, r2.astype(BF16)


def _dot(a, b):
    return jnp.dot(a, b, preferred_element_type=F32)


def _dot_nt(a, b):
    return lax.dot_general(a, b, (((1,), (1,)), ((), ())), preferred_element_type=F32)


def _dot_tn(a, b):
    return lax.dot_general(a, b, (((0,), (0,)), ((), ())), preferred_element_type=F32)


def _mod_kernel(c_ref, w_ref, b_ref, o_ref):
    c = c_ref[...]
    act = c / (1.0 + jnp.exp(-c))
    a_hi = act.astype(BF16)
    a_lo = (act - a_hi.astype(F32)).astype(BF16)
    w = w_ref[...]
    w_hi = w.astype(BF16)
    w_lo = (w - w_hi.astype(F32)).astype(BF16)
    o_ref[...] = _dot(a_hi, w_hi) + _dot(a_hi, w_lo) + _dot(a_lo, w_hi) + b_ref[...]


def _mod_call(c, w_mod, b_mod):
    b = c.shape[0]
    n = w_mod.shape[1]
    bn = D_MODEL
    return pl.pallas_call(
        _mod_kernel,
        grid=(n // bn,),
        in_specs=[
            pl.BlockSpec((b, D_MODEL), lambda j: (0, 0)),
            pl.BlockSpec((D_MODEL, bn), lambda j: (0, j)),
            pl.BlockSpec((1, bn), lambda j: (0, j)),
        ],
        out_specs=pl.BlockSpec((b, bn), lambda j: (0, j)),
        out_shape=jax.ShapeDtypeStruct((b, n), F32),
        compiler_params=pltpu.CompilerParams(dimension_semantics=("arbitrary",)),
        name="mod",
    )(c, w_mod, b_mod.reshape(1, n))


def _proj_kernel(x_ref, mod_ref, gain_ref, wnat_ref, wt_ref, bf_ref, pos_ref, e_ref,
                 ka_ref, kb_ref, ki_ref, t_ref, wi_ref, carry_ref, *, tm, q_scale):
    si = pl.program_id(1)

    @pl.when(si == 0)
    def _():
        carry_ref[...] = jnp.zeros_like(carry_ref)

    x = x_ref[0]
    ms = jnp.mean(x * x, axis=-1, keepdims=True)
    xn = (x * lax.rsqrt(ms + EPS)) * gain_ref[...]
    shift = mod_ref[0, 0:1, :]
    scale = mod_ref[0, 1:2, :]
    h = (xn * (1.0 + scale) + shift).astype(BF16)

    pn = _dot(h, wnat_ref[...])
    pt = _dot_nt(wt_ref[...], h)

    lane = lax.broadcasted_iota(I32, (tm, LANES), 1)
    fl = pn[:, NAT_F * LANES:(NAT_F + 1) * LANES] + bf_ref[...]
    logf = jnp.minimum(fl, 0.0) - jnp.log(1.0 + jnp.exp(-jnp.abs(fl)))
    logf = jnp.where(lane < N_HEADS, logf, 0.0)
    row = lax.broadcasted_iota(I32, (tm, tm), 0)
    col = lax.broadcasted_iota(I32, (tm, tm), 1)
    tri = jnp.where(col <= row, 1.0, 0.0).astype(BF16)
    l_hi, l_mid, l_lo = _split3(logf)
    cum = carry_ref[...] + (_dot(tri, l_hi) + _dot(tri, l_mid) + _dot(tri, l_lo))
    carry_ref[...] = cum[tm - 1:tm, :]

    f_hi, f_mid, f_lo = _split3(cum * (-LOG2E))
    pieces = (f_hi.astype(F32) + pltpu.roll(f_mid.astype(F32), N_HEADS, axis=1)
              + pltpu.roll(f_lo.astype(F32), 2 * N_HEADS, axis=1))
    faug = _dot(pieces.astype(BF16), e_ref[...])
    pos = pos_ref[...].astype(F32)
    low = lane < HEAD_DIM
    for pair in range(HEAD_PAIRS):
        for grp, out_ref, aug in ((NAT_KA, ka_ref, None), (NAT_KB, kb_ref, pos)):
            both = pn[:, (grp + pair) * LANES:(grp + pair + 1) * LANES]
            for odd in range(2):
                hh = 2 * pair + odd
                k_h = pltpu.roll(both, HEAD_DIM, axis=1) if odd else both
                tail = faug[:, hh * LANES:(hh + 1) * LANES] if aug is None else aug
                out_ref[0, hh] = jnp.where(low, k_h, tail).astype(BF16)
    ki_ref[0] = pn[:, NAT_KI * LANES:(NAT_KI + 1) * LANES].astype(BF16)

    def put(group, val):
        for cc in range(tm // TILE):
            t_ref[0, cc, group * GROUP_W:(group + 1) * GROUP_W, :] = (
                val[:, cc * TILE:(cc + 1) * TILE].astype(BF16))

    def rows(group):
        return pt[group * GROUP_W:(group + 1) * GROUP_W, :]

    def silu(v):
        return v / (1.0 + jnp.exp(-v))

    put(G_QA, rows(G_QA) * q_scale)
    put(G_VA, rows(G_VA))
    put(G_GA, silu(rows(G_GA)))
    put(G_QB, rows(G_QB) * q_scale)
    put(G_VB, rows(G_VB))
    put(G_GB, silu(rows(G_GB)))
    put(G_QI, rows(G_QI))
    wi_ref[0] = pt[N_TGROUPS * GROUP_W:N_TGROUPS * GROUP_W + N_HEADS, :]


def _proj_call(x, mod3, gain, w_nat, w_t, bf_row, pos_aug, e_mats, *, tm):
    b, s, _ = x.shape
    ns = s // tm
    nt = s // TILE
    t_rows = N_TGROUPS * GROUP_W
    kern = functools.partial(
        _proj_kernel, tm=tm,
        q_scale=float(HEAD_DIM ** -0.5 * LOG2E))
    const = dict(pipeline_mode=pl.Buffered(1))
    return pl.pallas_call(
        kern,
        grid=(b, ns),
        in_specs=[
            pl.BlockSpec((1, tm, D_MODEL), lambda i, j: (i, j, 0)),
            pl.BlockSpec((1, 3, D_MODEL), lambda i, j: (i, 0, 0)),
            pl.BlockSpec((1, D_MODEL), lambda i, j: (0, 0)),
            pl.BlockSpec((D_MODEL, N_NAT), lambda i, j: (0, 0), **const),
            pl.BlockSpec((t_rows + WI_ROWS, D_MODEL), lambda i, j: (0, 0), **const),
            pl.BlockSpec((1, LANES), lambda i, j: (0, 0)),
            pl.BlockSpec((tm, LANES), lambda i, j: (j, 0)),
            pl.BlockSpec((LANES, N_HEADS * LANES), lambda i, j: (0, 0), **const),
        ],
        out_specs=[
            pl.BlockSpec((1, N_HEADS, tm, LANES), lambda i, j: (i, 0, j, 0)),
            pl.BlockSpec((1, N_HEADS, tm, LANES), lambda i, j: (i, 0, j, 0)),
            pl.BlockSpec((1, tm, LANES), lambda i, j: (i, j, 0)),
            pl.BlockSpec((1, tm // TILE, t_rows, TILE), lambda i, j: (i, j, 0, 0)),
            pl.BlockSpec((1, N_HEADS, tm), lambda i, j: (i, 0, j)),
        ],
        out_shape=[
            jax.ShapeDtypeStruct((b, N_HEADS, s, LANES), BF16),
            jax.ShapeDtypeStruct((b, N_HEADS, s, LANES), BF16),
            jax.ShapeDtypeStruct((b, s, LANES), BF16),
            jax.ShapeDtypeStruct((b, nt, t_rows, TILE), BF16),
            jax.ShapeDtypeStruct((b, N_HEADS, s), F32),
        ],
        scratch_shapes=[pltpu.VMEM((1, LANES), F32)],
        compiler_params=pltpu.CompilerParams(
            dimension_semantics=("arbitrary", "arbitrary"), vmem_limit_bytes=VMEM_LIMIT),
        name="proj",
    )(x, mod3, gain, w_nat, w_t, bf_row, pos_aug, e_mats)


def _tile_rows(ki):
    if isinstance(ki, int):
        return pl.ds(ki * TILE, TILE)
    return pl.ds(pl.multiple_of(ki * TILE, TILE), TILE)


def _prev_tile(ki):
    return max(ki - 1, 0) if isinstance(ki, int) else jnp.maximum(ki - 1, 0)


def _attend(n_last, hb, scores, mask_last, v_tile):
    t = TILE
    ones_rows = jnp.ones((ACC_ROWS - HEAD_DIM, t), BF16)

    def pv(j, ki, alpha, acc, p):
        v_aug = jnp.concatenate([v_tile(j, ki), ones_rows], axis=0)
        upd = _dot(v_aug, p)
        return upd if acc is None else alpha * acc + upd

    def exp_tile(s, m, tile_max):
        m_new = jnp.maximum(m, tile_max)
        return m_new, jnp.exp2(m - m_new), jnp.exp2(s - m_new).astype(BF16)

    s = [scores(j, 0) for j in range(hb)]
    m = [jnp.full((1, t), NEG, F32) for _ in range(hb)]
    acc = [None] * hb
    for ki in range(n_last):
        for j in range(hb):
            m[j], alpha, p = exp_tile(s[j], m[j], jnp.max(s[j], axis=0, keepdims=True))
            s[j] = scores(j, ki + 1)
            acc[j] = pv(j, ki, alpha, acc[j], p)
    for j in range(hb):
        last = mask_last(s[j])
        m[j], alpha, p = exp_tile(last, m[j], jnp.max(last, axis=0, keepdims=True))
        acc[j] = pv(j, n_last, alpha, acc[j], p)
    return acc


def _attend_result(acc, gate):
    return ((acc[:HEAD_DIM] / acc[HEAD_DIM:HEAD_DIM + 1]) * gate.astype(F32)).astype(BF16)


def _fox_kernel(q_ref, k_ref, v_ref, g_ref, qaug_ref, o_ref, *, hb):
    qi = pl.program_id(2)
    t = TILE
    qz = [jnp.concatenate([q_ref[0, 0, j * HEAD_DIM:(j + 1) * HEAD_DIM, :], qaug_ref[...]], axis=0)
          for j in range(hb)]
    causal = lax.broadcasted_iota(I32, (t, t), 0) <= lax.broadcasted_iota(I32, (t, t), 1)

    def scores(j, ki):
        k_t = k_ref[0, j, _tile_rows(ki), :]
        return _dot(k_t, qz[j])

    def mask_last(s):
        return jnp.where(causal, s, NEG)

    def v_tile(j, ki):
        return v_ref[0, ki, j * HEAD_DIM:(j + 1) * HEAD_DIM, :]

    def run(n_last):
        acc = _attend(n_last, hb, scores, mask_last, v_tile)
        for j in range(hb):
            rows = slice(j * HEAD_DIM, (j + 1) * HEAD_DIM)
            o_ref[0, 0, rows, :] = _attend_result(acc[j], g_ref[0, 0, rows, :])

    for qv in range(v_ref.shape[1]):
        pl.when(qi == qv)(functools.partial(run, qv))


def _fox_call(ka, t_arr, qaug, *, hb):
    b, _, s, _ = ka.shape
    nt = s // TILE
    gpb = GROUP_W // (HEAD_DIM * hb)
    rb = HEAD_DIM * hb
    return pl.pallas_call(
        functools.partial(_fox_kernel, hb=hb),
        grid=(b, N_HEADS // hb, nt),
        in_specs=[
            pl.BlockSpec((1, 1, rb, TILE), lambda i, h, q: (i, q, G_QA * gpb + h, 0)),
            pl.BlockSpec((1, hb, s, LANES), lambda i, h, q: (i, h, 0, 0)),
            pl.BlockSpec((1, nt, rb, TILE), lambda i, h, q: (i, 0, G_VA * gpb + h, 0)),
            pl.BlockSpec((1, 1, rb, TILE), lambda i, h, q: (i, q, G_GA * gpb + h, 0)),
            pl.BlockSpec((HEAD_DIM, TILE), lambda i, h, q: (0, 0)),
        ],
        out_specs=pl.BlockSpec((1, 1, rb, TILE), lambda i, h, q: (i, q, h, 0)),
        out_shape=jax.ShapeDtypeStruct((b, nt, GROUP_W, TILE), BF16),
        compiler_params=pltpu.CompilerParams(
            dimension_semantics=("arbitrary", "arbitrary", "arbitrary"),
            vmem_limit_bytes=VMEM_LIMIT),
        name="fox",
    )(t_arr, ka, t_arr, t_arr, qaug)


def _dsa_kernel(qi_ref, wi_ref, ki_ref, q_ref, k_ref, v_ref, g_ref, qaug_ref, o_ref,
                sc_ref, sb_ref, hi_ref, lo_ref, bias_ref, thr_ref, lim_ref, cgt_ref, cge_ref,
                *, top_k, idx_bits, idx_scale, hb):
    qidx = pl.program_id(1)
    t = TILE
    nkv = qidx + 1
    srow = lax.broadcasted_iota(I32, (t, t), 0)
    tcol = lax.broadcasted_iota(I32, (t, t), 1)
    zpad = jnp.zeros((LANES - HEAD_DIM, t), BF16)

    qiz = [jnp.concatenate([qi_ref[0, 0, j * HEAD_DIM:(j + 1) * HEAD_DIM, :], zpad], axis=0)
           for j in range(N_HEADS)]
    wrow = [wi_ref[0, j:j + 1, :] for j in range(N_HEADS)]

    def index_scores(ki):
        k_t = ki_ref[0, _tile_rows(ki), :]
        sc = jnp.zeros((t, t), F32)
        for j in range(N_HEADS):
            sc = sc + wrow[j] * jnp.maximum(_dot(k_t, qiz[j]), 0.0)
        return sc * idx_scale

    def put_scores(ki, sc):
        sc_ref[ki] = sc
        sb_ref[ki] = sc.astype(BF16)
        bits = lax.bitcast_convert_type(sc, I32)
        key = bits ^ ((bits >> 31) & 0x7FFFFFFF)
        hi_ref[ki] = (key >> 16).astype(I16)
        lo_ref[ki] = ((key & 0xFFFF) - 32768).astype(I16)

    def index_tiles(n_last):
        for ki in range(n_last):
            put_scores(ki, index_scores(ki))
        put_scores(n_last, jnp.where(srow <= tcol, index_scores(n_last), -jnp.inf))

    for qv in range(v_ref.shape[1]):
        pl.when(qidx == qv)(functools.partial(index_tiles, qv))

    def count(pred):
        def body(ki, c):
            sidx = srow + ki * t
            ind = jnp.where(pred(sc_ref[ki], sidx), 1, 0).astype(I32)
            return c + jnp.sum(ind.reshape(t // 8, 8, t), axis=0)
        c8 = lax.fori_loop(0, nkv, body, jnp.zeros((8, t), I32))
        return jnp.sum(c8, axis=0, keepdims=True)

    def count_packed(ref, pred, one, zero):
        def body(ki, c):
            ind = jnp.where(pred(ref[ki]), one, zero)
            parts = [ind[r:r + 16] for r in range(0, t, 16)]
            while len(parts) > 1:
                parts = [x + y for x, y in zip(parts[0::2], parts[1::2])]
            return c + parts[0]
        c16 = lax.fori_loop(0, nkv, body, jnp.zeros((16, t), one.dtype))
        return jnp.sum(c16.astype(F32), axis=0, keepdims=True).astype(I32)

    def count16(ref, pred):
        return count_packed(ref, pred, jnp.ones((), I16), jnp.zeros((), I16))

    def count_rounded(cand):
        return count_packed(sb_ref, lambda v: v >= cand, jnp.ones((), BF16), jnp.zeros((), BF16))

    def key_to_f32(key):
        return lax.bitcast_convert_type(key ^ ((key >> 31) & 0x7FFFFFFF), F32)

    def kth_largest16(ref, kth):
        c0 = count16(ref, lambda v: v >= jnp.int16(0))
        val0 = jnp.where(c0 >= kth, 0, -32768).astype(I32)

        def bit_body(i, val):
            cand = val + lax.shift_left(jnp.int32(1), 14 - i)
            cand16 = cand.astype(I16)
            return jnp.where(count16(ref, lambda v: v >= cand16) >= kth, cand, val)

        return lax.fori_loop(0, 15, bit_body, val0)

    def store_threshold(thr):
        c_gt = count(lambda sc, sidx: sc > thr)
        c_ge = count(lambda sc, sidx: sc >= thr)
        thr_ref[...] = thr
        cgt_ref[...] = c_gt
        cge_ref[...] = c_ge
        ok = (c_gt < top_k) & ((c_ge >= top_k) | (thr == -jnp.inf))
        return jnp.max(jnp.where(ok, 0, 1))

    thr_ref[...] = jnp.full((1, t), -jnp.inf, F32)
    lim_ref[...] = jnp.zeros((1, t), I32)

    @pl.when(nkv * t > top_k)
    def _():
        hi_inf = -32641
        thr_hi = kth_largest16(hi_ref, top_k)
        thr_hi16 = thr_hi.astype(I16)
        above_hi = count16(hi_ref, lambda v: v > thr_hi16)

        def p2(ki, c):
            lo_ref[ki] = jnp.where(hi_ref[ki] == thr_hi16, lo_ref[ki], jnp.int16(-32768))
            return c

        lax.fori_loop(0, nkv, p2, 0)
        thr_lo = kth_largest16(lo_ref, top_k - above_hi)
        fast = jnp.where(thr_hi > hi_inf,
                         key_to_f32(lax.shift_left(thr_hi, 16) | (thr_lo + 32768)), -jnp.inf)
        wrong = store_threshold(fast)

        @pl.when(wrong > 0)
        def _():
            def coarse_value(key16):
                return key_to_f32(lax.shift_left(key16, 16)).astype(BF16)

            c0 = count_rounded(coarse_value(jnp.zeros((1, t), I32)))
            key0 = jnp.where(c0 >= top_k, 0, -32768).astype(I32)

            def coarse_body(i, key16):
                cand = key16 + lax.shift_left(jnp.int32(1), 14 - i)
                return jnp.where(count_rounded(coarse_value(cand)) >= top_k, cand, key16)

            key16 = lax.fori_loop(0, 15, coarse_body, key0)
            base = lax.shift_left(key16, 16) - 65536

            def fine_body(i, off):
                cand = off + lax.shift_left(jnp.int32(1), 16 - i)
                thr_c = key_to_f32(base + cand)
                return jnp.where(count(lambda sc, sidx: sc >= thr_c) >= top_k, cand, off)

            off = lax.fori_loop(0, 17, fine_body, jnp.zeros((1, t), I32))
            store_threshold(jnp.where(key16 > -32768, key_to_f32(base + off), -jnp.inf))

        thr = thr_ref[...]
        active = thr > -jnp.inf
        need = top_k - cgt_ref[...]
        big = jnp.int32(2 ** idx_bits - 1)
        lim_ref[...] = jnp.where(active, big, 0)
        excess = jnp.max(jnp.where(active & (cge_ref[...] > top_k), 1, 0))

        @pl.when(excess > 0)
        def _():
            def lim_body(i, lim):
                cand = lim + lax.shift_left(jnp.int32(1), idx_bits - 1 - i)
                c = count(lambda sc, sidx: (sc == thr) & (sidx < cand))
                return jnp.where(c <= need, cand, lim)

            lim = lax.fori_loop(0, idx_bits, lim_body, jnp.zeros((1, t), I32))
            lim_ref[...] = jnp.where(active, lim, 0)

    thr = thr_ref[...]
    lim = lim_ref[...]

    def p3(ki, c):
        sc = sc_ref[ki]
        sidx = srow + ki * t
        sel = (sc > thr) | ((sc == thr) & (sidx < lim))
        bias_ref[ki] = jnp.where(sel, 0.0, NEG).astype(F32)
        return c

    lax.fori_loop(0, nkv, p3, 0)

    def attend_heads(n_last):
        for j0 in range(0, N_HEADS, hb):
            qz = [jnp.concatenate([q_ref[0, 0, j * HEAD_DIM:(j + 1) * HEAD_DIM, :], qaug_ref[j]],
                                  axis=0) for j in range(j0, j0 + hb)]

            def scores(n, ki, j0=j0, qz=qz):
                k_t = k_ref[0, j0 + n, _tile_rows(ki), :]
                return _dot(k_t, qz[n]) + bias_ref[ki]

            def v_tile(n, ki, j0=j0):
                return v_ref[0, ki, (j0 + n) * HEAD_DIM:(j0 + n + 1) * HEAD_DIM, :]

            acc = _attend(n_last, hb, scores, lambda s: s, v_tile)
            for n, j in enumerate(range(j0, j0 + hb)):
                rows = slice(j * HEAD_DIM, (j + 1) * HEAD_DIM)
                o_ref[0, 0, rows, :] = _attend_result(acc[n], g_ref[0, 0, rows, :])

    for qv in range(v_ref.shape[1]):
        pl.when(qidx == qv)(functools.partial(attend_heads, qv))


def _dsa_call(t_arr, wi, ki, kb, qaug, *, hb):
    b, _, s, _ = kb.shape
    nt = s // TILE
    top_k = min(TOPK_MAX, s // 4)
    idx_bits = int(np.ceil(np.log2(s))) + 1
    kern = functools.partial(_dsa_kernel, top_k=top_k, idx_bits=idx_bits, hb=hb,
                             idx_scale=float(HEAD_DIM ** -0.5 * N_HEADS ** -0.5))
    return pl.pallas_call(
        kern,
        grid=(b, nt),
        in_specs=[
            pl.BlockSpec((1, 1, GROUP_W, TILE), lambda i, q: (i, q, G_QI, 0)),
            pl.BlockSpec((1, N_HEADS, TILE), lambda i, q: (i, 0, q)),
            pl.BlockSpec((1, s, LANES), lambda i, q: (i, 0, 0)),
            pl.BlockSpec((1, 1, GROUP_W, TILE), lambda i, q: (i, q, G_QB, 0)),
            pl.BlockSpec((1, N_HEADS, s, LANES), lambda i, q: (i, 0, 0, 0)),
            pl.BlockSpec((1, nt, GROUP_W, TILE), lambda i, q: (i, 0, G_VB, 0)),
            pl.BlockSpec((1, 1, GROUP_W, TILE), lambda i, q: (i, q, G_GB, 0)),
            pl.BlockSpec((N_HEADS, HEAD_DIM, TILE), lambda i, q: (0, 0, 0)),
        ],
        out_specs=pl.BlockSpec((1, 1, GROUP_W, TILE), lambda i, q: (i, q, 0, 0)),
        out_shape=jax.ShapeDtypeStruct((b, nt, GROUP_W, TILE), BF16),
        scratch_shapes=[
            pltpu.VMEM((nt, TILE, TILE), F32),
            pltpu.VMEM((nt, TILE, TILE), BF16),
            pltpu.VMEM((nt, TILE, TILE), I16),
            pltpu.VMEM((nt, TILE, TILE), I16),
            pltpu.VMEM((nt, TILE, TILE), F32),
            pltpu.VMEM((1, TILE), F32),
            pltpu.VMEM((1, TILE), I32),
            pltpu.VMEM((1, TILE), I32),
            pltpu.VMEM((1, TILE), I32),
        ],
        compiler_params=pltpu.CompilerParams(
            dimension_semantics=("arbitrary", "arbitrary"), vmem_limit_bytes=VMEM_LIMIT),
        name="dsa",
    )(t_arr, wi, ki, t_arr, kb, t_arr, t_arr, qaug)


def _out_kernel(x_ref, ya_ref, yb_ref, w_ref, gate_ref, gain_ref, o_ref, *, nsub):
    for cc in range(nsub):
        rows = slice(cc * TILE, (cc + 1) * TILE)
        y_t = jnp.concatenate([ya_ref[0, cc], yb_ref[0, cc]], axis=0)
        y = _dot_tn(y_t, w_ref[...])
        z = x_ref[0, rows, :] + gate_ref[0] * y
        ms = jnp.mean(z * z, axis=-1, keepdims=True)
        o_ref[0, rows, :] = (z * lax.rsqrt(ms + EPS)) * gain_ref[...]


def _out_call(x, ya, yb, w_out, gate3, final_gain, *, tm):
    b, s, _ = x.shape
    nsub = tm // TILE
    return pl.pallas_call(
        functools.partial(_out_kernel, nsub=nsub),
        grid=(b, s // tm),
        in_specs=[
            pl.BlockSpec((1, tm, D_MODEL), lambda i, q: (i, q, 0)),
            pl.BlockSpec((1, nsub, GROUP_W, TILE), lambda i, q: (i, q, 0, 0)),
            pl.BlockSpec((1, nsub, GROUP_W, TILE), lambda i, q: (i, q, 0, 0)),
            pl.BlockSpec((2 * GROUP_W, D_MODEL), lambda i, q: (0, 0)),
            pl.BlockSpec((1, 1, D_MODEL), lambda i, q: (i, 0, 0)),
            pl.BlockSpec((1, D_MODEL), lambda i, q: (0, 0)),
        ],
        out_specs=pl.BlockSpec((1, tm, D_MODEL), lambda i, q: (i, q, 0)),
        out_shape=jax.ShapeDtypeStruct((b, s, D_MODEL), F32),
        compiler_params=pltpu.CompilerParams(
            dimension_semantics=("arbitrary", "arbitrary"), vmem_limit_bytes=VMEM_LIMIT),
        name="outproj",
    )(x, ya, yb, w_out, gate3, final_gain)


def _layout_weights(w_in):
    widths = (GROUP_W, GROUP_W, GROUP_W, N_HEADS, GROUP_W, GROUP_W, GROUP_W, GROUP_W,
              N_HEADS * HEAD_DIM, HEAD_DIM, N_HEADS, GROUP_W)
    offs = np.concatenate([[0], np.cumsum(widths)])
    (w_qa, w_ka, w_va, w_f, w_ga, w_qb, w_kb, w_vb, w_qi, w_ki, w_wi, w_gb) = (
        w_in[:, offs[n]:offs[n + 1]] for n in range(len(widths)))

    def pad_cols(w):
        return jnp.pad(w, ((0, 0), (0, LANES - w.shape[1])))

    w_nat = jnp.concatenate([w_ka, w_kb, pad_cols(w_f), pad_cols(w_ki)], axis=1)
    w_t = jnp.concatenate([w_qa, w_va, w_ga, w_qb, w_vb, w_gb, w_qi,
                           jnp.pad(w_wi, ((0, 0), (0, WI_ROWS - N_HEADS)))], axis=1).T
    return w_nat.astype(BF16), w_t.astype(BF16)


def _constants(s):
    e = np.zeros((LANES, N_HEADS * LANES), np.float32)
    for i in range(3):
        for h in range(N_HEADS):
            e[N_HEADS * i + h, h * LANES + HEAD_DIM + i] = 1.0
    pos = np.zeros((s, LANES), np.float32)
    sidx = np.arange(s)
    for i in range(3):
        pos[:, HEAD_DIM + i] = sidx // 64
        pos[:, HEAD_DIM + 3 + i] = sidx % 64
    qa = np.zeros((HEAD_DIM, TILE), np.float32)
    qa[0:3, :] = 1.0
    return jnp.asarray(e, BF16), jnp.asarray(pos, BF16), jnp.asarray(qa, BF16)


def _alibi_rows():
    i = jnp.arange(1, N_HEADS + 1, dtype=F32)
    slopes = jnp.exp2(-8.0 * i / N_HEADS) * LOG2E
    c64 = _split3(slopes * 64.0)
    c1 = _split3(slopes)
    rows = jnp.stack(list(c64) + list(c1), axis=1)
    rows = jnp.pad(rows, ((0, 0), (0, HEAD_DIM - 6)))
    return jnp.broadcast_to(rows[:, :, None], (N_HEADS, HEAD_DIM, TILE)).astype(BF16)


def kernel(x, c, w_mod, b_mod, norm_gain, w_in, b_forget, w_out, final_gain):
    b, s, d = x.shape
    assert d == D_MODEL and s % TILE == 0
    assert w_mod.shape[0] == 1, "single-layer block"
    tm = ROW_BLOCK if s % ROW_BLOCK == 0 else TILE

    mod = _mod_call(c, w_mod[0], b_mod[0])
    mod3 = mod.reshape(b, 3, D_MODEL)
    w_nat, w_t = _layout_weights(w_in[0])
    e_mats, pos_aug, qaug_a = _constants(s)
    bf_row = jnp.pad(b_forget[0], (0, LANES - N_HEADS)).reshape(1, LANES)

    ka, kb, ki, t_arr, wi = _proj_call(
        x, mod3, norm_gain[0].reshape(1, D_MODEL), w_nat, w_t, bf_row, pos_aug, e_mats, tm=tm)

    ya = _fox_call(ka, t_arr, qaug_a, hb=HEADS_PER_BODY)
    yb = _dsa_call(t_arr, wi, ki, kb, _alibi_rows(), hb=HEADS_PER_BODY)

    gate3 = mod3[:, 2:3, :]
    return _out_call(x, ya, yb, w_out[0].astype(BF16), gate3, final_gain.reshape(1, D_MODEL), tm=tm)
```

```python
import functools

import jax
import jax.numpy as jnp
import numpy as np
from jax import lax
from jax.experimental import pallas as pl
from jax.experimental.pallas import tpu as pltpu

F32 = jnp.float32
BF16 = jnp.bfloat16
I32 = jnp.int32
I16 = jnp.int16

D_MODEL = 1024
HEAD_DIM = 64
N_HEADS = 8
GROUP_W = N_HEADS * HEAD_DIM
TOPK_MAX = 256
EPS = 1e-6
LOG2E = 1.4426950408889634
NEG = -1e30

LANES = 128
TILE = 256
ROW_BLOCK = 512
HEADS_PER_BODY = 8
ACC_ROWS = HEAD_DIM + 16
VMEM_LIMIT = 56 * 1024 * 1024

G_QA, G_VA, G_GA, G_QB, G_VB, G_GB, G_QI = range(7)
N_TGROUPS = 7
WI_ROWS = 16
HEAD_PAIRS = N_HEADS * HEAD_DIM // LANES
NAT_KA = 0
NAT_KB = HEAD_PAIRS
NAT_F = 2 * HEAD_PAIRS
NAT_KI = 2 * HEAD_PAIRS + 1
N_NAT = (2 * HEAD_PAIRS + 2) * LANES


def _split3(v):
    hi = v.astype(BF16)
    r1 = v - hi.astype(F32)
    mid = r1.astype(BF16)
    r2 = r1 - mid.astype(F32)
    return hi, mid, r2.astype(BF16)


def _dot(a, b):
    return jnp.dot(a, b, preferred_element_type=F32)


def _dot_nt(a, b):
    return lax.dot_general(a, b, (((1,), (1,)), ((), ())), preferred_element_type=F32)


def _dot_tn(a, b):
    return lax.dot_general(a, b, (((0,), (0,)), ((), ())), preferred_element_type=F32)


def _mod_kernel(c_ref, w_ref, b_ref, o_ref):
    c = c_ref[...]
    act = c / (1.0 + jnp.exp(-c))
    a_hi = act.astype(BF16)
    a_lo = (act - a_hi.astype(F32)).astype(BF16)
    w = w_ref[...]
    w_hi = w.astype(BF16)
    w_lo = (w - w_hi.astype(F32)).astype(BF16)
    o_ref[...] = _dot(a_hi, w_hi) + _dot(a_hi, w_lo) + _dot(a_lo, w_hi) + b_ref[...]


def _mod_call(c, w_mod, b_mod):
    b = c.shape[0]
    n = w_mod.shape[1]
    bn = D_MODEL
    return pl.pallas_call(
        _mod_kernel,
        grid=(n // bn,),
        in_specs=[
            pl.BlockSpec((b, D_MODEL), lambda j: (0, 0)),
            pl.BlockSpec((D_MODEL, bn), lambda j: (0, j)),
            pl.BlockSpec((1, bn), lambda j: (0, j)),
        ],
        out_specs=pl.BlockSpec((b, bn), lambda j: (0, j)),
        out_shape=jax.ShapeDtypeStruct((b, n), F32),
        compiler_params=pltpu.CompilerParams(dimension_semantics=("arbitrary",)),
        name="mod",
    )(c, w_mod, b_mod.reshape(1, n))


def _proj_kernel(x_ref, mod_ref, gain_ref, wnat_ref, wt_ref, bf_ref, pos_ref, e_ref,
                 ka_ref, kb_ref, ki_ref, t_ref, wi_ref, carry_ref, *, tm, q_scale):
    si = pl.program_id(1)

    @pl.when(si == 0)
    def _():
        carry_ref[...] = jnp.zeros_like(carry_ref)

    x = x_ref[0]
    ms = jnp.mean(x * x, axis=-1, keepdims=True)
    xn = (x * lax.rsqrt(ms + EPS)) * gain_ref[...]
    shift = mod_ref[0, 0:1, :]
    scale = mod_ref[0, 1:2, :]
    h = (xn * (1.0 + scale) + shift).astype(BF16)

    pn = _dot(h, wnat_ref[...])
    pt = _dot_nt(wt_ref[...], h)

    lane = lax.broadcasted_iota(I32, (tm, LANES), 1)
    row = lax.broadcasted_iota(I32, (tm, LANES), 0)
    fl = pn[:, NAT_F * LANES:(NAT_F + 1) * LANES] + bf_ref[...]
    logf = jnp.minimum(fl, 0.0) - jnp.log(1.0 + jnp.exp(-jnp.abs(fl)))
    cum = jnp.where(lane < N_HEADS, logf, 0.0)
    step = 1
    while step < tm:
        cum = cum + jnp.where(row >= step, pltpu.roll(cum, step, axis=0), 0.0)
        step *= 2
    cum = carry_ref[...] + cum
    carry_ref[...] = cum[tm - 1:tm, :]

    f_hi, f_mid, f_lo = _split3(cum * (-LOG2E))
    pieces = (f_hi.astype(F32) + pltpu.roll(f_mid.astype(F32), N_HEADS, axis=1)
              + pltpu.roll(f_lo.astype(F32), 2 * N_HEADS, axis=1))
    faug = _dot(pieces.astype(BF16), e_ref[...])
    pos = pos_ref[...].astype(F32)
    low = lane < HEAD_DIM
    for pair in range(HEAD_PAIRS):
        for grp, out_ref, aug in ((NAT_KA, ka_ref, None), (NAT_KB, kb_ref, pos)):
            both = pn[:, (grp + pair) * LANES:(grp + pair + 1) * LANES]
            for odd in range(2):
                hh = 2 * pair + odd
                k_h = pltpu.roll(both, HEAD_DIM, axis=1) if odd else both
                tail = faug[:, hh * LANES:(hh + 1) * LANES] if aug is None else aug
                out_ref[0, hh] = jnp.where(low, k_h, tail).astype(BF16)
    ki_ref[0] = pn[:, NAT_KI * LANES:(NAT_KI + 1) * LANES].astype(BF16)

    def put(group, val):
        for cc in range(tm // TILE):
            t_ref[0, cc, group * GROUP_W:(group + 1) * GROUP_W, :] = (
                val[:, cc * TILE:(cc + 1) * TILE].astype(BF16))

    def rows(group):
        return pt[group * GROUP_W:(group + 1) * GROUP_W, :]

    def silu(v):
        return v / (1.0 + jnp.exp(-v))

    put(G_QA, rows(G_QA) * q_scale)
    put(G_VA, rows(G_VA))
    put(G_GA, silu(rows(G_GA)))
    put(G_QB, rows(G_QB) * q_scale)
    put(G_VB, rows(G_VB))
    put(G_GB, silu(rows(G_GB)))
    put(G_QI, rows(G_QI))
    wi_ref[0] = pt[N_TGROUPS * GROUP_W:N_TGROUPS * GROUP_W + N_HEADS, :]


def _proj_call(x, mod3, gain, w_nat, w_t, bf_row, pos_aug, e_mats, *, tm):
    b, s, _ = x.shape
    ns = s // tm
    nt = s // TILE
    t_rows = N_TGROUPS * GROUP_W
    kern = functools.partial(
        _proj_kernel, tm=tm,
        q_scale=float(HEAD_DIM ** -0.5 * LOG2E))
    const = dict(pipeline_mode=pl.Buffered(1))
    return pl.pallas_call(
        kern,
        grid=(b, ns),
        in_specs=[
            pl.BlockSpec((1, tm, D_MODEL), lambda i, j: (i, j, 0)),
            pl.BlockSpec((1, 3, D_MODEL), lambda i, j: (i, 0, 0)),
            pl.BlockSpec((1, D_MODEL), lambda i, j: (0, 0)),
            pl.BlockSpec((D_MODEL, N_NAT), lambda i, j: (0, 0), **const),
            pl.BlockSpec((t_rows + WI_ROWS, D_MODEL), lambda i, j: (0, 0), **const),
            pl.BlockSpec((1, LANES), lambda i, j: (0, 0)),
            pl.BlockSpec((tm, LANES), lambda i, j: (j, 0)),
            pl.BlockSpec((LANES, N_HEADS * LANES), lambda i, j: (0, 0), **const),
        ],
        out_specs=[
            pl.BlockSpec((1, N_HEADS, tm, LANES), lambda i, j: (i, 0, j, 0)),
            pl.BlockSpec((1, N_HEADS, tm, LANES), lambda i, j: (i, 0, j, 0)),
            pl.BlockSpec((1, tm, LANES), lambda i, j: (i, j, 0)),
            pl.BlockSpec((1, tm // TILE, t_rows, TILE), lambda i, j: (i, j, 0, 0)),
            pl.BlockSpec((1, N_HEADS, tm), lambda i, j: (i, 0, j)),
        ],
        out_shape=[
            jax.ShapeDtypeStruct((b, N_HEADS, s, LANES), BF16),
            jax.ShapeDtypeStruct((b, N_HEADS, s, LANES), BF16),
            jax.ShapeDtypeStruct((b, s, LANES), BF16),
            jax.ShapeDtypeStruct((b, nt, t_rows, TILE), BF16),
            jax.ShapeDtypeStruct((b, N_HEADS, s), F32),
        ],
        scratch_shapes=[pltpu.VMEM((1, LANES), F32)],
        compiler_params=pltpu.CompilerParams(
            dimension_semantics=("arbitrary", "arbitrary"), vmem_limit_bytes=VMEM_LIMIT),
        name="proj",
    )(x, mod3, gain, w_nat, w_t, bf_row, pos_aug, e_mats)


def _tile_rows(ki):
    if isinstance(ki, int):
        return pl.ds(ki * TILE, TILE)
    return pl.ds(pl.multiple_of(ki * TILE, TILE), TILE)


def _attend(n_last, hb, scores, mask_last, v_tile):
    t = TILE
    ones_rows = jnp.ones((ACC_ROWS - HEAD_DIM, t), BF16)

    def pv(j, ki, alpha, acc, p):
        v_aug = jnp.concatenate([v_tile(j, ki), ones_rows], axis=0)
        upd = _dot(v_aug, p)
        return upd if acc is None else alpha * acc + upd

    def exp_tile(s, m, tile_max):
        m_new = jnp.maximum(m, tile_max)
        return m_new, jnp.exp2(m - m_new), jnp.exp2(s - m_new).astype(BF16)

    s = [scores(j, 0) for j in range(hb)]
    m = [jnp.full((1, t), NEG, F32) for _ in range(hb)]
    acc = [None] * hb
    for ki in range(n_last):
        for j in range(hb):
            m[j], alpha, p = exp_tile(s[j], m[j], jnp.max(s[j], axis=0, keepdims=True))
            s[j] = scores(j, ki + 1)
            acc[j] = pv(j, ki, alpha, acc[j], p)
    for j in range(hb):
        last = mask_last(s[j])
        m[j], alpha, p = exp_tile(last, m[j], jnp.max(last, axis=0, keepdims=True))
        acc[j] = pv(j, n_last, alpha, acc[j], p)
    return acc


def _attend_result(acc, gate):
    return ((acc[:HEAD_DIM] / acc[HEAD_DIM:HEAD_DIM + 1]) * gate.astype(F32)).astype(BF16)


def _fox_kernel(q_ref, k_ref, v_ref, g_ref, qaug_ref, o_ref, *, hb):
    qi = pl.program_id(2)
    t = TILE
    qz = [jnp.concatenate([q_ref[0, 0, j * HEAD_DIM:(j + 1) * HEAD_DIM, :], qaug_ref[...]], axis=0)
          for j in range(hb)]
    causal = lax.broadcasted_iota(I32, (t, t), 0) <= lax.broadcasted_iota(I32, (t, t), 1)

    def scores(j, ki):
        k_t = k_ref[0, j, _tile_rows(ki), :]
        return _dot(k_t, qz[j])

    def mask_last(s):
        return jnp.where(causal, s, NEG)

    def v_tile(j, ki):
        return v_ref[0, ki, j * HEAD_DIM:(j + 1) * HEAD_DIM, :]

    def run(n_last):
        acc = _attend(n_last, hb, scores, mask_last, v_tile)
        for j in range(hb):
            rows = slice(j * HEAD_DIM, (j + 1) * HEAD_DIM)
            o_ref[0, 0, rows, :] = _attend_result(acc[j], g_ref[0, 0, rows, :])

    for qv in range(v_ref.shape[1]):
        pl.when(qi == qv)(functools.partial(run, qv))


def _fox_call(ka, t_arr, qaug, *, hb):
    b, _, s, _ = ka.shape
    nt = s // TILE
    gpb = GROUP_W // (HEAD_DIM * hb)
    rb = HEAD_DIM * hb
    return pl.pallas_call(
        functools.partial(_fox_kernel, hb=hb),
        grid=(b, N_HEADS // hb, nt),
        in_specs=[
            pl.BlockSpec((1, 1, rb, TILE), lambda i, h, q: (i, q, G_QA * gpb + h, 0)),
            pl.BlockSpec((1, hb, s, LANES), lambda i, h, q: (i, h, 0, 0)),
            pl.BlockSpec((1, nt, rb, TILE), lambda i, h, q: (i, 0, G_VA * gpb + h, 0)),
            pl.BlockSpec((1, 1, rb, TILE), lambda i, h, q: (i, q, G_GA * gpb + h, 0)),
            pl.BlockSpec((HEAD_DIM, TILE), lambda i, h, q: (0, 0)),
        ],
        out_specs=pl.BlockSpec((1, 1, rb, TILE), lambda i, h, q: (i, q, h, 0)),
        out_shape=jax.ShapeDtypeStruct((b, nt, GROUP_W, TILE), BF16),
        compiler_params=pltpu.CompilerParams(
            dimension_semantics=("arbitrary", "arbitrary", "arbitrary"),
            vmem_limit_bytes=VMEM_LIMIT),
        name="fox",
    )(t_arr, ka, t_arr, t_arr, qaug)


def _dsa_kernel(qi_ref, wi_ref, ki_ref, q_ref, k_ref, v_ref, g_ref, qaug_ref, o_ref,
                sc_ref, sb_ref, hi_ref, lo_ref, bias_ref, thr_ref, lim_ref, cgt_ref, cge_ref,
                *, top_k, idx_bits, idx_scale, hb):
    qidx = pl.program_id(1)
    t = TILE
    nkv = qidx + 1
    srow = lax.broadcasted_iota(I32, (t, t), 0)
    tcol = lax.broadcasted_iota(I32, (t, t), 1)
    zpad = jnp.zeros((LANES - HEAD_DIM, t), BF16)

    qiz = [jnp.concatenate([qi_ref[0, 0, j * HEAD_DIM:(j + 1) * HEAD_DIM, :], zpad], axis=0)
           for j in range(N_HEADS)]
    wrow = [wi_ref[0, j:j + 1, :] for j in range(N_HEADS)]

    def index_scores(ki):
        k_t = ki_ref[0, _tile_rows(ki), :]
        sc = jnp.zeros((t, t), F32)
        for j in range(N_HEADS):
            sc = sc + wrow[j] * jnp.maximum(_dot(k_t, qiz[j]), 0.0)
        return sc * idx_scale

    def put_scores(ki, sc):
        sc_ref[ki] = sc
        sb_ref[ki] = sc.astype(BF16)
        bits = lax.bitcast_convert_type(sc, I32)
        key = bits ^ ((bits >> 31) & 0x7FFFFFFF)
        hi_ref[ki] = (key >> 16).astype(I16)
        lo_ref[ki] = ((key & 0xFFFF) - 32768).astype(I16)

    def index_tiles(n_last):
        for ki in range(n_last):
            put_scores(ki, index_scores(ki))
        put_scores(n_last, jnp.where(srow <= tcol, index_scores(n_last), -jnp.inf))

    for qv in range(v_ref.shape[1]):
        pl.when(qidx == qv)(functools.partial(index_tiles, qv))

    def count(pred):
        def body(ki, c):
            sidx = srow + ki * t
            ind = jnp.where(pred(sc_ref[ki], sidx), 1, 0).astype(I32)
            return c + jnp.sum(ind.reshape(t // 8, 8, t), axis=0)
        c8 = lax.fori_loop(0, nkv, body, jnp.zeros((8, t), I32))
        return jnp.sum(c8, axis=0, keepdims=True)

    def count_packed(ref, pred, one, zero):
        def body(ki, c):
            ind = jnp.where(pred(ref[ki]), one, zero)
            parts = [ind[r:r + 16] for r in range(0, t, 16)]
            while len(parts) > 1:
                parts = [x + y for x, y in zip(parts[0::2], parts[1::2])]
            return c + parts[0]
        c16 = lax.fori_loop(0, nkv, body, jnp.zeros((16, t), one.dtype))
        return jnp.sum(c16.astype(F32), axis=0, keepdims=True).astype(I32)

    def count16(ref, pred):
        return count_packed(ref, pred, jnp.ones((), I16), jnp.zeros((), I16))

    def count_rounded(cand):
        return count_packed(sb_ref, lambda v: v >= cand, jnp.ones((), BF16), jnp.zeros((), BF16))

    def key_to_f32(key):
        return lax.bitcast_convert_type(key ^ ((key >> 31) & 0x7FFFFFFF), F32)

    def kth_largest16(ref, kth):
        c0 = count16(ref, lambda v: v >= jnp.int16(0))
        val0 = jnp.where(c0 >= kth, 0, -32768).astype(I32)

        def bit_body(i, val):
            cand = val + lax.shift_left(jnp.int32(1), 14 - i)
            cand16 = cand.astype(I16)
            return jnp.where(count16(ref, lambda v: v >= cand16) >= kth, cand, val)

        return lax.fori_loop(0, 15, bit_body, val0)

    def store_threshold(thr):
        c_gt = count(lambda sc, sidx: sc > thr)
        c_ge = count(lambda sc, sidx: sc >= thr)
        thr_ref[...] = thr
        cgt_ref[...] = c_gt
        cge_ref[...] = c_ge
        ok = (c_gt < top_k) & ((c_ge >= top_k) | (thr == -jnp.inf))
        return jnp.max(jnp.where(ok, 0, 1))

    thr_ref[...] = jnp.full((1, t), -jnp.inf, F32)
    lim_ref[...] = jnp.zeros((1, t), I32)

    @pl.when(nkv * t > top_k)
    def _():
        hi_inf = -32641
        thr_hi = kth_largest16(hi_ref, top_k)
        thr_hi16 = thr_hi.astype(I16)
        above_hi = count16(hi_ref, lambda v: v > thr_hi16)

        def p2(ki, c):
            lo_ref[ki] = jnp.where(hi_ref[ki] == thr_hi16, lo_ref[ki], jnp.int16(-32768))
            return c

        lax.fori_loop(0, nkv, p2, 0)
        thr_lo = kth_largest16(lo_ref, top_k - above_hi)
        fast = jnp.where(thr_hi > hi_inf,
                         key_to_f32(lax.shift_left(thr_hi, 16) | (thr_lo + 32768)), -jnp.inf)
        wrong = store_threshold(fast)

        @pl.when(wrong > 0)
        def _():
            def coarse_value(key16):
                return key_to_f32(lax.shift_left(key16, 16)).astype(BF16)

            c0 = count_rounded(coarse_value(jnp.zeros((1, t), I32)))
            key0 = jnp.where(c0 >= top_k, 0, -32768).astype(I32)

            def coarse_body(i, key16):
                cand = key16 + lax.shift_left(jnp.int32(1), 14 - i)
                return jnp.where(count_rounded(coarse_value(cand)) >= top_k, cand, key16)

            key16 = lax.fori_loop(0, 15, coarse_body, key0)
            base = lax.shift_left(key16, 16) - 65536

            def fine_body(i, off):
                cand = off + lax.shift_left(jnp.int32(1), 16 - i)
                thr_c = key_to_f32(base + cand)
                return jnp.where(count(lambda sc, sidx: sc >= thr_c) >= top_k, cand, off)

            off = lax.fori_loop(0, 17, fine_body, jnp.zeros((1, t), I32))
            store_threshold(jnp.where(key16 > -32768, key_to_f32(base + off), -jnp.inf))

        thr = thr_ref[...]
        active = thr > -jnp.inf
        need = top_k - cgt_ref[...]
        big = jnp.int32(2 ** idx_bits - 1)
        lim_ref[...] = jnp.where(active, big, 0)
        excess = jnp.max(jnp.where(active & (cge_ref[...] > top_k), 1, 0))

        @pl.when(excess > 0)
        def _():
            def lim_body(i, lim):
                cand = lim + lax.shift_left(jnp.int32(1), idx_bits - 1 - i)
                c = count(lambda sc, sidx: (sc == thr) & (sidx < cand))
                return jnp.where(c <= need, cand, lim)

            lim = lax.fori_loop(0, idx_bits, lim_body, jnp.zeros((1, t), I32))
            lim_ref[...] = jnp.where(active, lim, 0)

    thr = thr_ref[...]
    lim = lim_ref[...]

    def p3(ki, c):
        sc = sc_ref[ki]
        sidx = srow + ki * t
        sel = (sc > thr) | ((sc == thr) & (sidx < lim))
        bias_ref[ki] = jnp.where(sel, 0.0, NEG).astype(F32)
        return c

    lax.fori_loop(0, nkv, p3, 0)

    def attend_heads(n_last):
        for j0 in range(0, N_HEADS, hb):
            qz = [jnp.concatenate([q_ref[0, 0, j * HEAD_DIM:(j + 1) * HEAD_DIM, :], qaug_ref[j]],
                                  axis=0) for j in range(j0, j0 + hb)]

            def scores(n, ki, j0=j0, qz=qz):
                k_t = k_ref[0, j0 + n, _tile_rows(ki), :]
                return _dot(k_t, qz[n]) + bias_ref[ki]

            def v_tile(n, ki, j0=j0):
                return v_ref[0, ki, (j0 + n) * HEAD_DIM:(j0 + n + 1) * HEAD_DIM, :]

            acc = _attend(n_last, hb, scores, lambda s: s, v_tile)
            for n, j in enumerate(range(j0, j0 + hb)):
                rows = slice(j * HEAD_DIM, (j + 1) * HEAD_DIM)
                o_ref[0, 0, rows, :] = _attend_result(acc[n], g_ref[0, 0, rows, :])

    for qv in range(v_ref.shape[1]):
        pl.when(qidx == qv)(functools.partial(attend_heads, qv))


def _dsa_call(t_arr, wi, ki, kb, qaug, *, hb):
    b, _, s, _ = kb.shape
    nt = s // TILE
    top_k = min(TOPK_MAX, s // 4)
    idx_bits = int(np.ceil(np.log2(s))) + 1
    kern = functools.partial(_dsa_kernel, top_k=top_k, idx_bits=idx_bits, hb=hb,
                             idx_scale=float(HEAD_DIM ** -0.5 * N_HEADS ** -0.5))
    return pl.pallas_call(
        kern,
        grid=(b, nt),
        in_specs=[
            pl.BlockSpec((1, 1, GROUP_W, TILE), lambda i, q: (i, q, G_QI, 0)),
            pl.BlockSpec((1, N_HEADS, TILE), lambda i, q: (i, 0, q)),
            pl.BlockSpec((1, s, LANES), lambda i, q: (i, 0, 0)),
            pl.BlockSpec((1, 1, GROUP_W, TILE), lambda i, q: (i, q, G_QB, 0)),
            pl.BlockSpec((1, N_HEADS, s, LANES), lambda i, q: (i, 0, 0, 0)),
            pl.BlockSpec((1, nt, GROUP_W, TILE), lambda i, q: (i, 0, G_VB, 0)),
            pl.BlockSpec((1, 1, GROUP_W, TILE), lambda i, q: (i, q, G_GB, 0)),
            pl.BlockSpec((N_HEADS, HEAD_DIM, TILE), lambda i, q: (0, 0, 0)),
        ],
        out_specs=pl.BlockSpec((1, 1, GROUP_W, TILE), lambda i, q: (i, q, 0, 0)),
        out_shape=jax.ShapeDtypeStruct((b, nt, GROUP_W, TILE), BF16),
        scratch_shapes=[
            pltpu.VMEM((nt, TILE, TILE), F32),
            pltpu.VMEM((nt, TILE, TILE), BF16),
            pltpu.VMEM((nt, TILE, TILE), I16),
            pltpu.VMEM((nt, TILE, TILE), I16),
            pltpu.VMEM((nt, TILE, TILE), F32),
            pltpu.VMEM((1, TILE), F32),
            pltpu.VMEM((1, TILE), I32),
            pltpu.VMEM((1, TILE), I32),
            pltpu.VMEM((1, TILE), I32),
        ],
        compiler_params=pltpu.CompilerParams(
            dimension_semantics=("arbitrary", "arbitrary"), vmem_limit_bytes=VMEM_LIMIT),
        name="dsa",
    )(t_arr, wi, ki, t_arr, kb, t_arr, t_arr, qaug)


def _out_kernel(x_ref, ya_ref, yb_ref, w_ref, gate_ref, gain_ref, o_ref, *, nsub):
    for cc in range(nsub):
        rows = slice(cc * TILE, (cc + 1) * TILE)
        y_t = jnp.concatenate([ya_ref[0, cc], yb_ref[0, cc]], axis=0)
        y = _dot_tn(y_t, w_ref[...])
        z = x_ref[0, rows, :] + gate_ref[0] * y
        ms = jnp.mean(z * z, axis=-1, keepdims=True)
        o_ref[0, rows, :] = (z * lax.rsqrt(ms + EPS)) * gain_ref[...]


def _out_call(x, ya, yb, w_out, gate3, final_gain, *, tm):
    b, s, _ = x.shape
    nsub = tm // TILE
    return pl.pallas_call(
        functools.partial(_out_kernel, nsub=nsub),
        grid=(b, s // tm),
        in_specs=[
            pl.BlockSpec((1, tm, D_MODEL), lambda i, q: (i, q, 0)),
            pl.BlockSpec((1, nsub, GROUP_W, TILE), lambda i, q: (i, q, 0, 0)),
            pl.BlockSpec((1, nsub, GROUP_W, TILE), lambda i, q: (i, q, 0, 0)),
            pl.BlockSpec((2 * GROUP_W, D_MODEL), lambda i, q: (0, 0)),
            pl.BlockSpec((1, 1, D_MODEL), lambda i, q: (i, 0, 0)),
            pl.BlockSpec((1, D_MODEL), lambda i, q: (0, 0)),
        ],
        out_specs=pl.BlockSpec((1, tm, D_MODEL), lambda i, q: (i, q, 0)),
        out_shape=jax.ShapeDtypeStruct((b, s, D_MODEL), F32),
        compiler_params=pltpu.CompilerParams(
            dimension_semantics=("arbitrary", "arbitrary"), vmem_limit_bytes=VMEM_LIMIT),
        name="outproj",
    )(x, ya, yb, w_out, gate3, final_gain)


def _layout_weights(w_in):
    widths = (GROUP_W, GROUP_W, GROUP_W, N_HEADS, GROUP_W, GROUP_W, GROUP_W, GROUP_W,
              N_HEADS * HEAD_DIM, HEAD_DIM, N_HEADS, GROUP_W)
    offs = np.concatenate([[0], np.cumsum(widths)])
    (w_qa, w_ka, w_va, w_f, w_ga, w_qb, w_kb, w_vb, w_qi, w_ki, w_wi, w_gb) = (
        w_in[:, offs[n]:offs[n + 1]] for n in range(len(widths)))

    def pad_cols(w):
        return jnp.pad(w, ((0, 0), (0, LANES - w.shape[1])))

    w_nat = jnp.concatenate([w_ka, w_kb, pad_cols(w_f), pad_cols(w_ki)], axis=1)
    w_t = jnp.concatenate([w_qa, w_va, w_ga, w_qb, w_vb, w_gb, w_qi,
                           jnp.pad(w_wi, ((0, 0), (0, WI_ROWS - N_HEADS)))], axis=1).T
    return w_nat.astype(BF16), w_t.astype(BF16)


def _constants(s):
    e = np.zeros((LANES, N_HEADS * LANES), np.float32)
    for i in range(3):
        for h in range(N_HEADS):
            e[N_HEADS * i + h, h * LANES + HEAD_DIM + i] = 1.0
    pos = np.zeros((s, LANES), np.float32)
    sidx = np.arange(s)
    for i in range(3):
        pos[:, HEAD_DIM + i] = sidx // 64
        pos[:, HEAD_DIM + 3 + i] = sidx % 64
    qa = np.zeros((HEAD_DIM, TILE), np.float32)
    qa[0:3, :] = 1.0
    return jnp.asarray(e, BF16), jnp.asarray(pos, BF16), jnp.asarray(qa, BF16)


def _alibi_rows():
    i = jnp.arange(1, N_HEADS + 1, dtype=F32)
    slopes = jnp.exp2(-8.0 * i / N_HEADS) * LOG2E
    c64 = _split3(slopes * 64.0)
    c1 = _split3(slopes)
    rows = jnp.stack(list(c64) + list(c1), axis=1)
    rows = jnp.pad(rows, ((0, 0), (0, HEAD_DIM - 6)))
    return jnp.broadcast_to(rows[:, :, None], (N_HEADS, HEAD_DIM, TILE)).astype(BF16)


def kernel(x, c, w_mod, b_mod, norm_gain, w_in, b_forget, w_out, final_gain):
    b, s, d = x.shape
    assert d == D_MODEL and s % TILE == 0
    assert w_mod.shape[0] == 1, "single-layer block"
    tm = ROW_BLOCK if s % ROW_BLOCK == 0 else TILE

    mod = _mod_call(c, w_mod[0], b_mod[0])
    mod3 = mod.reshape(b, 3, D_MODEL)
    w_nat, w_t = _layout_weights(w_in[0])
    e_mats, pos_aug, qaug_a = _constants(s)
    bf_row = jnp.pad(b_forget[0], (0, LANES - N_HEADS)).reshape(1, LANES)

    ka, kb, ki, t_arr, wi = _proj_call(
        x, mod3, norm_gain[0].reshape(1, D_MODEL), w_nat, w_t, bf_row, pos_aug, e_mats, tm=tm)

    ya = _fox_call(ka, t_arr, qaug_a, hb=HEADS_PER_BODY)
    yb = _dsa_call(t_arr, wi, ki, kb, _alibi_rows(), hb=HEADS_PER_BODY)

    gate3 = mod3[:, 2:3, :]
    return _out_call(x, ya, yb, w_out[0].astype(BF16), gate3, final_gain.reshape(1, D_MODEL), tm=tm)
```

```python
import functools

import jax
import jax.numpy as jnp
import numpy as np
from jax import lax
from jax.experimental import pallas as pl
from jax.experimental.pallas import tpu as pltpu

F32 = jnp.float32
BF16 = jnp.bfloat16
I32 = jnp.int32
I16 = jnp.int16

D_MODEL = 1024
HEAD_DIM = 64
N_HEADS = 8
GROUP_W = N_HEADS * HEAD_DIM
TOPK_MAX = 256
EPS = 1e-6
LOG2E = 1.4426950408889634
NEG = -1e30

LANES = 128
TILE = 256
ROW_BLOCK = 512
OUT_ROW_BLOCK = 1024
HEADS_PER_BODY = 8
ACC_ROWS = HEAD_DIM + 16
VMEM_LIMIT = 56 * 1024 * 1024

G_QA, G_VA, G_GA, G_QB, G_VB, G_GB, G_QI = range(7)
N_TGROUPS = 7
WI_ROWS = 16
HEAD_PAIRS = N_HEADS * HEAD_DIM // LANES
NAT_KA = 0
NAT_KB = HEAD_PAIRS
NAT_F = 2 * HEAD_PAIRS
NAT_KI = 2 * HEAD_PAIRS + 1
N_NAT = (2 * HEAD_PAIRS + 2) * LANES


def _split3(v):
    hi = v.astype(BF16)
    r1 = v - hi.astype(F32)
    mid = r1.astype(BF16)
    r2 = r1 - mid.astype(F32)
    return hi, mid, r2.astype(BF16)


def _dot(a, b):
    return jnp.dot(a, b, preferred_element_type=F32)


def _dot_nt(a, b):
    return lax.dot_general(a, b, (((1,), (1,)), ((), ())), preferred_element_type=F32)


def _dot_tn(a, b):
    return lax.dot_general(a, b, (((0,), (0,)), ((), ())), preferred_element_type=F32)


def _mod_kernel(c_ref, w_ref, b_ref, o_ref):
    c = c_ref[...]
    act = c / (1.0 + jnp.exp(-c))
    a_hi = act.astype(BF16)
    a_lo = (act - a_hi.astype(F32)).astype(BF16)
    w = w_ref[...]
    w_hi = w.astype(BF16)
    w_lo = (w - w_hi.astype(F32)).astype(BF16)
    o_ref[...] = _dot(a_hi, w_hi) + _dot(a_hi, w_lo) + _dot(a_lo, w_hi) + b_ref[...]


def _mod_call(c, w_mod, b_mod):
    b = c.shape[0]
    n = w_mod.shape[1]
    bn = D_MODEL
    return pl.pallas_call(
        _mod_kernel,
        grid=(n // bn,),
        in_specs=[
            pl.BlockSpec((b, D_MODEL), lambda j: (0, 0)),
            pl.BlockSpec((D_MODEL, bn), lambda j: (0, j)),
            pl.BlockSpec((1, bn), lambda j: (0, j)),
        ],
        out_specs=pl.BlockSpec((b, bn), lambda j: (0, j)),
        out_shape=jax.ShapeDtypeStruct((b, n), F32),
        compiler_params=pltpu.CompilerParams(dimension_semantics=("arbitrary",)),
        name="mod",
    )(c, w_mod, b_mod.reshape(1, n))


def _proj_kernel(x_ref, mod_ref, gain_ref, wnat_ref, wt_ref, bf_ref, pos_ref, e_ref,
                 ka_ref, kb_ref, ki_ref, t_ref, wi_ref, carry_ref, *, tm, q_scale):
    si = pl.program_id(1)

    @pl.when(si == 0)
    def _():
        carry_ref[...] = jnp.zeros_like(carry_ref)

    x = x_ref[0]
    ms = jnp.mean(x * x, axis=-1, keepdims=True)
    xn = (x * lax.rsqrt(ms + EPS)) * gain_ref[...]
    shift = mod_ref[0, 0:1, :]
    scale = mod_ref[0, 1:2, :]
    h = (xn * (1.0 + scale) + shift).astype(BF16)

    pn = _dot(h, wnat_ref[...])
    pt = _dot_nt(wt_ref[...], h)

    lane = lax.broadcasted_iota(I32, (tm, LANES), 1)
    row = lax.broadcasted_iota(I32, (tm, LANES), 0)
    fl = pn[:, NAT_F * LANES:(NAT_F + 1) * LANES] + bf_ref[...]
    logf = jnp.minimum(fl, 0.0) - jnp.log(1.0 + jnp.exp(-jnp.abs(fl)))
    cum = jnp.where(lane < N_HEADS, logf, 0.0)
    step = 1
    while step < tm:
        cum = cum + jnp.where(row >= step, pltpu.roll(cum, step, axis=0), 0.0)
        step *= 2
    cum = carry_ref[...] + cum
    carry_ref[...] = cum[tm - 1:tm, :]

    f_hi, f_mid, f_lo = _split3(cum * (-LOG2E))
    pieces = (f_hi.astype(F32) + pltpu.roll(f_mid.astype(F32), N_HEADS, axis=1)
              + pltpu.roll(f_lo.astype(F32), 2 * N_HEADS, axis=1))
    faug = _dot(pieces.astype(BF16), e_ref[...])
    pos = pos_ref[...].astype(F32)
    low = lane < HEAD_DIM
    for pair in range(HEAD_PAIRS):
        for grp, out_ref, aug in ((NAT_KA, ka_ref, None), (NAT_KB, kb_ref, pos)):
            both = pn[:, (grp + pair) * LANES:(grp + pair + 1) * LANES]
            for odd in range(2):
                hh = 2 * pair + odd
                k_h = pltpu.roll(both, HEAD_DIM, axis=1) if odd else both
                tail = faug[:, hh * LANES:(hh + 1) * LANES] if aug is None else aug
                out_ref[0, hh] = jnp.where(low, k_h, tail).astype(BF16)
    ki_ref[0] = pn[:, NAT_KI * LANES:(NAT_KI + 1) * LANES].astype(BF16)

    def put(group, val):
        for cc in range(tm // TILE):
            t_ref[0, cc, group * GROUP_W:(group + 1) * GROUP_W, :] = (
                val[:, cc * TILE:(cc + 1) * TILE].astype(BF16))

    def rows(group):
        return pt[group * GROUP_W:(group + 1) * GROUP_W, :]

    def silu(v):
        return v / (1.0 + jnp.exp(-v))

    put(G_QA, rows(G_QA) * q_scale)
    put(G_VA, rows(G_VA))
    put(G_GA, silu(rows(G_GA)))
    put(G_QB, rows(G_QB) * q_scale)
    put(G_VB, rows(G_VB))
    put(G_GB, silu(rows(G_GB)))
    put(G_QI, rows(G_QI))
    wi_ref[0] = pt[N_TGROUPS * GROUP_W:N_TGROUPS * GROUP_W + N_HEADS, :]


def _proj_call(x, mod3, gain, w_nat, w_t, bf_row, pos_aug, e_mats, *, tm):
    b, s, _ = x.shape
    ns = s // tm
    nt = s // TILE
    t_rows = N_TGROUPS * GROUP_W
    kern = functools.partial(
        _proj_kernel, tm=tm,
        q_scale=float(HEAD_DIM ** -0.5 * LOG2E))
    const = dict(pipeline_mode=pl.Buffered(1))
    return pl.pallas_call(
        kern,
        grid=(b, ns),
        in_specs=[
            pl.BlockSpec((1, tm, D_MODEL), lambda i, j: (i, j, 0)),
            pl.BlockSpec((1, 3, D_MODEL), lambda i, j: (i, 0, 0)),
            pl.BlockSpec((1, D_MODEL), lambda i, j: (0, 0)),
            pl.BlockSpec((D_MODEL, N_NAT), lambda i, j: (0, 0), **const),
            pl.BlockSpec((t_rows + WI_ROWS, D_MODEL), lambda i, j: (0, 0), **const),
            pl.BlockSpec((1, LANES), lambda i, j: (0, 0)),
            pl.BlockSpec((tm, LANES), lambda i, j: (j, 0)),
            pl.BlockSpec((LANES, N_HEADS * LANES), lambda i, j: (0, 0), **const),
        ],
        out_specs=[
            pl.BlockSpec((1, N_HEADS, tm, LANES), lambda i, j: (i, 0, j, 0)),
            pl.BlockSpec((1, N_HEADS, tm, LANES), lambda i, j: (i, 0, j, 0)),
            pl.BlockSpec((1, tm, LANES), lambda i, j: (i, j, 0)),
            pl.BlockSpec((1, tm // TILE, t_rows, TILE), lambda i, j: (i, j, 0, 0)),
            pl.BlockSpec((1, N_HEADS, tm), lambda i, j: (i, 0, j)),
        ],
        out_shape=[
            jax.ShapeDtypeStruct((b, N_HEADS, s, LANES), BF16),
            jax.ShapeDtypeStruct((b, N_HEADS, s, LANES), BF16),
            jax.ShapeDtypeStruct((b, s, LANES), BF16),
            jax.ShapeDtypeStruct((b, nt, t_rows, TILE), BF16),
            jax.ShapeDtypeStruct((b, N_HEADS, s), F32),
        ],
        scratch_shapes=[pltpu.VMEM((1, LANES), F32)],
        compiler_params=pltpu.CompilerParams(
            dimension_semantics=("arbitrary", "arbitrary"), vmem_limit_bytes=VMEM_LIMIT),
        name="proj",
    )(x, mod3, gain, w_nat, w_t, bf_row, pos_aug, e_mats)


def _tile_rows(ki):
    if isinstance(ki, int):
        return pl.ds(ki * TILE, TILE)
    return pl.ds(pl.multiple_of(ki * TILE, TILE), TILE)


def _attend(n_last, hb, scores, mask_last, v_tile):
    t = TILE
    ones_rows = jnp.ones((ACC_ROWS - HEAD_DIM, t), BF16)

    def pv(j, ki, alpha, acc, p):
        v_aug = jnp.concatenate([v_tile(j, ki), ones_rows], axis=0)
        upd = _dot(v_aug, p)
        return upd if acc is None else alpha * acc + upd

    def exp_tile(s, m, tile_max):
        m_new = jnp.maximum(m, tile_max)
        return m_new, jnp.exp2(m - m_new), jnp.exp2(s - m_new).astype(BF16)

    s = [scores(j, 0) for j in range(hb)]
    m = [jnp.full((1, t), NEG, F32) for _ in range(hb)]
    acc = [None] * hb
    for ki in range(n_last):
        for j in range(hb):
            m[j], alpha, p = exp_tile(s[j], m[j], jnp.max(s[j], axis=0, keepdims=True))
            s[j] = scores(j, ki + 1)
            acc[j] = pv(j, ki, alpha, acc[j], p)
    for j in range(hb):
        last = mask_last(s[j])
        m[j], alpha, p = exp_tile(last, m[j], jnp.max(last, axis=0, keepdims=True))
        acc[j] = pv(j, n_last, alpha, acc[j], p)
    return acc


def _attend_result(acc, gate):
    return ((acc[:HEAD_DIM] / acc[HEAD_DIM:HEAD_DIM + 1]) * gate.astype(F32)).astype(BF16)


def _fox_kernel(q_ref, k_ref, v_ref, g_ref, qaug_ref, o_ref, *, hb):
    qi = pl.program_id(2)
    t = TILE
    qz = [jnp.concatenate([q_ref[0, 0, j * HEAD_DIM:(j + 1) * HEAD_DIM, :], qaug_ref[...]], axis=0)
          for j in range(hb)]
    causal = lax.broadcasted_iota(I32, (t, t), 0) <= lax.broadcasted_iota(I32, (t, t), 1)

    def scores(j, ki):
        k_t = k_ref[0, j, _tile_rows(ki), :]
        return _dot(k_t, qz[j])

    def mask_last(s):
        return jnp.where(causal, s, NEG)

    def v_tile(j, ki):
        return v_ref[0, ki, j * HEAD_DIM:(j + 1) * HEAD_DIM, :]

    def run(n_last):
        acc = _attend(n_last, hb, scores, mask_last, v_tile)
        for j in range(hb):
            rows = slice(j * HEAD_DIM, (j + 1) * HEAD_DIM)
            o_ref[0, 0, rows, :] = _attend_result(acc[j], g_ref[0, 0, rows, :])

    for qv in range(v_ref.shape[1]):
        pl.when(qi == qv)(functools.partial(run, qv))


def _fox_call(ka, t_arr, qaug, *, hb):
    b, _, s, _ = ka.shape
    nt = s // TILE
    gpb = GROUP_W // (HEAD_DIM * hb)
    rb = HEAD_DIM * hb
    return pl.pallas_call(
        functools.partial(_fox_kernel, hb=hb),
        grid=(b, N_HEADS // hb, nt),
        in_specs=[
            pl.BlockSpec((1, 1, rb, TILE), lambda i, h, q: (i, q, G_QA * gpb + h, 0)),
            pl.BlockSpec((1, hb, s, LANES), lambda i, h, q: (i, h, 0, 0)),
            pl.BlockSpec((1, nt, rb, TILE), lambda i, h, q: (i, 0, G_VA * gpb + h, 0)),
            pl.BlockSpec((1, 1, rb, TILE), lambda i, h, q: (i, q, G_GA * gpb + h, 0)),
            pl.BlockSpec((HEAD_DIM, TILE), lambda i, h, q: (0, 0)),
        ],
        out_specs=pl.BlockSpec((1, 1, rb, TILE), lambda i, h, q: (i, q, h, 0)),
        out_shape=jax.ShapeDtypeStruct((b, nt, GROUP_W, TILE), BF16),
        compiler_params=pltpu.CompilerParams(
            dimension_semantics=("arbitrary", "arbitrary", "arbitrary"),
            vmem_limit_bytes=VMEM_LIMIT),
        name="fox",
    )(t_arr, ka, t_arr, t_arr, qaug)


def _dsa_kernel(qi_ref, wi_ref, ki_ref, q_ref, k_ref, v_ref, g_ref, qaug_ref, o_ref,
                sc_ref, sb_ref, hi_ref, lo_ref, bias_ref, thr_ref, lim_ref, cgt_ref, cge_ref,
                *, top_k, idx_bits, idx_scale, hb):
    qidx = pl.program_id(1)
    t = TILE
    nkv = qidx + 1
    srow = lax.broadcasted_iota(I32, (t, t), 0)
    tcol = lax.broadcasted_iota(I32, (t, t), 1)
    zpad = jnp.zeros((LANES - HEAD_DIM, t), BF16)

    qiz = [jnp.concatenate([qi_ref[0, 0, j * HEAD_DIM:(j + 1) * HEAD_DIM, :], zpad], axis=0)
           for j in range(N_HEADS)]
    wrow = [wi_ref[0, j:j + 1, :] for j in range(N_HEADS)]

    def index_scores(ki):
        k_t = ki_ref[0, _tile_rows(ki), :]
        sc = jnp.zeros((t, t), F32)
        for j in range(N_HEADS):
            sc = sc + wrow[j] * jnp.maximum(_dot(k_t, qiz[j]), 0.0)
        return sc * idx_scale

    def put_scores(ki, sc):
        sc_ref[ki] = sc
        sb_ref[ki] = sc.astype(BF16)
        bits = lax.bitcast_convert_type(sc, I32)
        key = bits ^ ((bits >> 31) & 0x7FFFFFFF)
        hi_ref[ki] = (key >> 16).astype(I16)
        lo_ref[ki] = ((key & 0xFFFF) - 32768).astype(I16)

    def index_tiles(n_last):
        for ki in range(n_last):
            put_scores(ki, index_scores(ki))
        put_scores(n_last, jnp.where(srow <= tcol, index_scores(n_last), -jnp.inf))

    for qv in range(v_ref.shape[1]):
        pl.when(qidx == qv)(functools.partial(index_tiles, qv))

    def count(pred):
        def body(ki, c):
            sidx = srow + ki * t
            ind = jnp.where(pred(sc_ref[ki], sidx), 1, 0).astype(I32)
            return c + jnp.sum(ind.reshape(t // 8, 8, t), axis=0)
        c8 = lax.fori_loop(0, nkv, body, jnp.zeros((8, t), I32))
        return jnp.sum(c8, axis=0, keepdims=True)

    def count_packed(ref, pred, one, zero):
        def body(ki, c):
            ind = jnp.where(pred(ref[ki]), one, zero)
            parts = [ind[r:r + 16] for r in range(0, t, 16)]
            while len(parts) > 1:
                parts = [x + y for x, y in zip(parts[0::2], parts[1::2])]
            return c + parts[0]
        c16 = lax.fori_loop(0, nkv, body, jnp.zeros((16, t), one.dtype))
        return jnp.sum(c16.astype(F32), axis=0, keepdims=True).astype(I32)

    def count16(ref, pred):
        return count_packed(ref, pred, jnp.ones((), I16), jnp.zeros((), I16))

    def count_rounded(cand):
        return count_packed(sb_ref, lambda v: v >= cand, jnp.ones((), BF16), jnp.zeros((), BF16))

    def key_to_f32(key):
        return lax.bitcast_convert_type(key ^ ((key >> 31) & 0x7FFFFFFF), F32)

    def kth_largest16(ref, kth):
        c0 = count16(ref, lambda v: v >= jnp.int16(0))
        val0 = jnp.where(c0 >= kth, 0, -32768).astype(I32)

        def bit_body(i, val):
            cand = val + lax.shift_left(jnp.int32(1), 14 - i)
            cand16 = cand.astype(I16)
            return jnp.where(count16(ref, lambda v: v >= cand16) >= kth, cand, val)

        return lax.fori_loop(0, 15, bit_body, val0)

    def store_threshold(thr):
        def body(ki, c):
            sc = sc_ref[ki]
            gt = jnp.where(sc > thr, 1, 0).astype(I32).reshape(t // 8, 8, t)
            ge = jnp.where(sc >= thr, 1, 0).astype(I32).reshape(t // 8, 8, t)
            return c[0] + jnp.sum(gt, axis=0), c[1] + jnp.sum(ge, axis=0)
        zero8 = jnp.zeros((8, t), I32)
        gt8, ge8 = lax.fori_loop(0, nkv, body, (zero8, zero8))
        c_gt = jnp.sum(gt8, axis=0, keepdims=True)
        c_ge = jnp.sum(ge8, axis=0, keepdims=True)
        thr_ref[...] = thr
        cgt_ref[...] = c_gt
        cge_ref[...] = c_ge
        ok = (c_gt < top_k) & ((c_ge >= top_k) | (thr == -jnp.inf))
        return jnp.max(jnp.where(ok, 0, 1))

    thr_ref[...] = jnp.full((1, t), -jnp.inf, F32)
    lim_ref[...] = jnp.zeros((1, t), I32)

    @pl.when(nkv * t > top_k)
    def _():
        hi_inf = -32641
        thr_hi = kth_largest16(hi_ref, top_k)
        thr_hi16 = thr_hi.astype(I16)

        def p2(ki, c):
            hi = hi_ref[ki]
            lo_ref[ki] = jnp.where(hi > thr_hi16, jnp.int16(32767),
                                   jnp.where(hi == thr_hi16, lo_ref[ki], jnp.int16(-32768)))
            return c

        lax.fori_loop(0, nkv, p2, 0)
        thr_lo = kth_largest16(lo_ref, top_k)
        fast = jnp.where(thr_hi > hi_inf,
                         key_to_f32(lax.shift_left(thr_hi, 16) | (thr_lo + 32768)), -jnp.inf)
        wrong = store_threshold(fast)

        @pl.when(wrong > 0)
        def _():
            def coarse_value(key16):
                return key_to_f32(lax.shift_left(key16, 16)).astype(BF16)

            c0 = count_rounded(coarse_value(jnp.zeros((1, t), I32)))
            key0 = jnp.where(c0 >= top_k, 0, -32768).astype(I32)

            def coarse_body(i, key16):
                cand = key16 + lax.shift_left(jnp.int32(1), 14 - i)
                return jnp.where(count_rounded(coarse_value(cand)) >= top_k, cand, key16)

            key16 = lax.fori_loop(0, 15, coarse_body, key0)
            base = lax.shift_left(key16, 16) - 65536

            def fine_body(i, off):
                cand = off + lax.shift_left(jnp.int32(1), 16 - i)
                thr_c = key_to_f32(base + cand)
                return jnp.where(count(lambda sc, sidx: sc >= thr_c) >= top_k, cand, off)

            off = lax.fori_loop(0, 17, fine_body, jnp.zeros((1, t), I32))
            store_threshold(jnp.where(key16 > -32768, key_to_f32(base + off), -jnp.inf))

        thr = thr_ref[...]
        active = thr > -jnp.inf
        need = top_k - cgt_ref[...]
        big = jnp.int32(2 ** idx_bits - 1)
        lim_ref[...] = jnp.where(active, big, 0)
        excess = jnp.max(jnp.where(active & (cge_ref[...] > top_k), 1, 0))

        @pl.when(excess > 0)
        def _():
            def lim_body(i, lim):
                cand = lim + lax.shift_left(jnp.int32(1), idx_bits - 1 - i)
                c = count(lambda sc, sidx: (sc == thr) & (sidx < cand))
                return jnp.where(c <= need, cand, lim)

            lim = lax.fori_loop(0, idx_bits, lim_body, jnp.zeros((1, t), I32))
            lim_ref[...] = jnp.where(active, lim, 0)

    thr = thr_ref[...]
    lim = lim_ref[...]

    def p3(ki, c):
        sc = sc_ref[ki]
        sidx = srow + ki * t
        sel = (sc > thr) | ((sc == thr) & (sidx < lim))
        bias_ref[ki] = jnp.where(sel, 0.0, NEG).astype(F32)
        return c

    lax.fori_loop(0, nkv, p3, 0)

    def attend_heads(n_last):
        for j0 in range(0, N_HEADS, hb):
            qz = [jnp.concatenate([q_ref[0, 0, j * HEAD_DIM:(j + 1) * HEAD_DIM, :], qaug_ref[j]],
                                  axis=0) for j in range(j0, j0 + hb)]

            def scores(n, ki, j0=j0, qz=qz):
                k_t = k_ref[0, j0 + n, _tile_rows(ki), :]
                return _dot(k_t, qz[n]) + bias_ref[ki]

            def v_tile(n, ki, j0=j0):
                return v_ref[0, ki, (j0 + n) * HEAD_DIM:(j0 + n + 1) * HEAD_DIM, :]

            acc = _attend(n_last, hb, scores, lambda s: s, v_tile)
            for n, j in enumerate(range(j0, j0 + hb)):
                rows = slice(j * HEAD_DIM, (j + 1) * HEAD_DIM)
                o_ref[0, 0, rows, :] = _attend_result(acc[n], g_ref[0, 0, rows, :])

    for qv in range(v_ref.shape[1]):
        pl.when(qidx == qv)(functools.partial(attend_heads, qv))


def _dsa_call(t_arr, wi, ki, kb, qaug, *, hb):
    b, _, s, _ = kb.shape
    nt = s // TILE
    top_k = min(TOPK_MAX, s // 4)
    idx_bits = int(np.ceil(np.log2(s))) + 1
    kern = functools.partial(_dsa_kernel, top_k=top_k, idx_bits=idx_bits, hb=hb,
                             idx_scale=float(HEAD_DIM ** -0.5 * N_HEADS ** -0.5))
    return pl.pallas_call(
        kern,
        grid=(b, nt),
        in_specs=[
            pl.BlockSpec((1, 1, GROUP_W, TILE), lambda i, q: (i, q, G_QI, 0)),
            pl.BlockSpec((1, N_HEADS, TILE), lambda i, q: (i, 0, q)),
            pl.BlockSpec((1, s, LANES), lambda i, q: (i, 0, 0)),
            pl.BlockSpec((1, 1, GROUP_W, TILE), lambda i, q: (i, q, G_QB, 0)),
            pl.BlockSpec((1, N_HEADS, s, LANES), lambda i, q: (i, 0, 0, 0)),
            pl.BlockSpec((1, nt, GROUP_W, TILE), lambda i, q: (i, 0, G_VB, 0)),
            pl.BlockSpec((1, 1, GROUP_W, TILE), lambda i, q: (i, q, G_GB, 0)),
            pl.BlockSpec((N_HEADS, HEAD_DIM, TILE), lambda i, q: (0, 0, 0)),
        ],
        out_specs=pl.BlockSpec((1, 1, GROUP_W, TILE), lambda i, q: (i, q, 0, 0)),
        out_shape=jax.ShapeDtypeStruct((b, nt, GROUP_W, TILE), BF16),
        scratch_shapes=[
            pltpu.VMEM((nt, TILE, TILE), F32),
            pltpu.VMEM((nt, TILE, TILE), BF16),
            pltpu.VMEM((nt, TILE, TILE), I16),
            pltpu.VMEM((nt, TILE, TILE), I16),
            pltpu.VMEM((nt, TILE, TILE), F32),
            pltpu.VMEM((1, TILE), F32),
            pltpu.VMEM((1, TILE), I32),
            pltpu.VMEM((1, TILE), I32),
            pltpu.VMEM((1, TILE), I32),
        ],
        compiler_params=pltpu.CompilerParams(
            dimension_semantics=("arbitrary", "arbitrary"), vmem_limit_bytes=VMEM_LIMIT),
        name="dsa",
    )(t_arr, wi, ki, t_arr, kb, t_arr, t_arr, qaug)


def _out_kernel(x_ref, ya_ref, yb_ref, w_ref, gate_ref, gain_ref, o_ref, *, nsub):
    for cc in range(nsub):
        rows = slice(cc * TILE, (cc + 1) * TILE)
        y_t = jnp.concatenate([ya_ref[0, cc], yb_ref[0, cc]], axis=0)
        y = _dot_tn(y_t, w_ref[...])
        z = x_ref[0, rows, :] + gate_ref[0] * y
        ms = jnp.mean(z * z, axis=-1, keepdims=True)
        o_ref[0, rows, :] = (z * lax.rsqrt(ms + EPS)) * gain_ref[...]


def _out_call(x, ya, yb, w_out, gate3, final_gain, *, tm):
    b, s, _ = x.shape
    nsub = tm // TILE
    return pl.pallas_call(
        functools.partial(_out_kernel, nsub=nsub),
        grid=(b, s // tm),
        in_specs=[
            pl.BlockSpec((1, tm, D_MODEL), lambda i, q: (i, q, 0)),
            pl.BlockSpec((1, nsub, GROUP_W, TILE), lambda i, q: (i, q, 0, 0)),
            pl.BlockSpec((1, nsub, GROUP_W, TILE), lambda i, q: (i, q, 0, 0)),
            pl.BlockSpec((2 * GROUP_W, D_MODEL), lambda i, q: (0, 0)),
            pl.BlockSpec((1, 1, D_MODEL), lambda i, q: (i, 0, 0)),
            pl.BlockSpec((1, D_MODEL), lambda i, q: (0, 0)),
        ],
        out_specs=pl.BlockSpec((1, tm, D_MODEL), lambda i, q: (i, q, 0)),
        out_shape=jax.ShapeDtypeStruct((b, s, D_MODEL), F32),
        compiler_params=pltpu.CompilerParams(
            dimension_semantics=("arbitrary", "arbitrary"), vmem_limit_bytes=VMEM_LIMIT),
        name="outproj",
    )(x, ya, yb, w_out, gate3, final_gain)


def _layout_weights(w_in):
    widths = (GROUP_W, GROUP_W, GROUP_W, N_HEADS, GROUP_W, GROUP_W, GROUP_W, GROUP_W,
              N_HEADS * HEAD_DIM, HEAD_DIM, N_HEADS, GROUP_W)
    offs = np.concatenate([[0], np.cumsum(widths)])
    (w_qa, w_ka, w_va, w_f, w_ga, w_qb, w_kb, w_vb, w_qi, w_ki, w_wi, w_gb) = (
        w_in[:, offs[n]:offs[n + 1]] for n in range(len(widths)))

    def pad_cols(w):
        return jnp.pad(w, ((0, 0), (0, LANES - w.shape[1])))

    w_nat = jnp.concatenate([w_ka, w_kb, pad_cols(w_f), pad_cols(w_ki)], axis=1)
    w_t = jnp.concatenate([w_qa, w_va, w_ga, w_qb, w_vb, w_gb, w_qi,
                           jnp.pad(w_wi, ((0, 0), (0, WI_ROWS - N_HEADS)))], axis=1).T
    return w_nat.astype(BF16), w_t.astype(BF16)


def _constants(s):
    e = np.zeros((LANES, N_HEADS * LANES), np.float32)
    for i in range(3):
        for h in range(N_HEADS):
            e[N_HEADS * i + h, h * LANES + HEAD_DIM + i] = 1.0
    pos = np.zeros((s, LANES), np.float32)
    sidx = np.arange(s)
    for i in range(3):
        pos[:, HEAD_DIM + i] = sidx // 64
        pos[:, HEAD_DIM + 3 + i] = sidx % 64
    qa = np.zeros((HEAD_DIM, TILE), np.float32)
    qa[0:3, :] = 1.0
    return jnp.asarray(e, BF16), jnp.asarray(pos, BF16), jnp.asarray(qa, BF16)


def _alibi_rows():
    i = jnp.arange(1, N_HEADS + 1, dtype=F32)
    slopes = jnp.exp2(-8.0 * i / N_HEADS) * LOG2E
    c64 = _split3(slopes * 64.0)
    c1 = _split3(slopes)
    rows = jnp.stack(list(c64) + list(c1), axis=1)
    rows = jnp.pad(rows, ((0, 0), (0, HEAD_DIM - 6)))
    return jnp.broadcast_to(rows[:, :, None], (N_HEADS, HEAD_DIM, TILE)).astype(BF16)


def kernel(x, c, w_mod, b_mod, norm_gain, w_in, b_forget, w_out, final_gain):
    b, s, d = x.shape
    assert d == D_MODEL and s % TILE == 0
    assert w_mod.shape[0] == 1, "single-layer block"
    tm = ROW_BLOCK if s % ROW_BLOCK == 0 else TILE

    mod = _mod_call(c, w_mod[0], b_mod[0])
    mod3 = mod.reshape(b, 3, D_MODEL)
    w_nat, w_t = _layout_weights(w_in[0])
    e_mats, pos_aug, qaug_a = _constants(s)
    bf_row = jnp.pad(b_forget[0], (0, LANES - N_HEADS)).reshape(1, LANES)

    ka, kb, ki, t_arr, wi = _proj_call(
        x, mod3, norm_gain[0].reshape(1, D_MODEL), w_nat, w_t, bf_row, pos_aug, e_mats, tm=tm)

    ya = _fox_call(ka, t_arr, qaug_a, hb=HEADS_PER_BODY)
    yb = _dsa_call(t_arr, wi, ki, kb, _alibi_rows(), hb=HEADS_PER_BODY)

    gate3 = mod3[:, 2:3, :]
    tm_out = OUT_ROW_BLOCK if s % OUT_ROW_BLOCK == 0 else tm
    return _out_call(x, ya, yb, w_out[0].astype(BF16), gate3, final_gain.reshape(1, D_MODEL), tm=tm_out)
```

```python
import functools

import jax
import jax.numpy as jnp
import numpy as np
from jax import lax
from jax.experimental import pallas as pl
from jax.experimental.pallas import tpu as pltpu

F32 = jnp.float32
BF16 = jnp.bfloat16
I32 = jnp.int32
I16 = jnp.int16

D_MODEL = 1024
HEAD_DIM = 64
N_HEADS = 8
GROUP_W = N_HEADS * HEAD_DIM
TOPK_MAX = 256
EPS = 1e-6
LOG2E = 1.4426950408889634
NEG = -1e30

LANES = 128
TILE = 256
ROW_BLOCK = 512
OUT_ROW_BLOCK = 1024
HEADS_PER_BODY = 8
ACC_ROWS = HEAD_DIM + 16
VMEM_LIMIT = 56 * 1024 * 1024

G_QA, G_VA, G_GA, G_QB, G_VB, G_GB, G_QI = range(7)
N_TGROUPS = 7
WI_ROWS = 16
HEAD_PAIRS = N_HEADS * HEAD_DIM // LANES
NAT_KA = 0
NAT_KB = HEAD_PAIRS
NAT_F = 2 * HEAD_PAIRS
NAT_KI = 2 * HEAD_PAIRS + 1
N_NAT = (2 * HEAD_PAIRS + 2) * LANES


def _split3(v):
    hi = v.astype(BF16)
    r1 = v - hi.astype(F32)
    mid = r1.astype(BF16)
    r2 = r1 - mid.astype(F32)
    return hi, mid, r2.astype(BF16)


def _dot(a, b):
    return jnp.dot(a, b, preferred_element_type=F32)


def _dot_nt(a, b):
    return lax.dot_general(a, b, (((1,), (1,)), ((), ())), preferred_element_type=F32)


def _dot_tn(a, b):
    return lax.dot_general(a, b, (((0,), (0,)), ((), ())), preferred_element_type=F32)


def _mod_kernel(c_ref, w_ref, b_ref, o_ref):
    c = c_ref[...]
    act = c / (1.0 + jnp.exp(-c))
    a_hi = act.astype(BF16)
    a_lo = (act - a_hi.astype(F32)).astype(BF16)
    w = w_ref[...]
    w_hi = w.astype(BF16)
    w_lo = (w - w_hi.astype(F32)).astype(BF16)
    o_ref[...] = _dot(a_hi, w_hi) + _dot(a_hi, w_lo) + _dot(a_lo, w_hi) + b_ref[...]


def _mod_call(c, w_mod, b_mod):
    b = c.shape[0]
    n = w_mod.shape[1]
    bn = D_MODEL
    return pl.pallas_call(
        _mod_kernel,
        grid=(n // bn,),
        in_specs=[
            pl.BlockSpec((b, D_MODEL), lambda j: (0, 0)),
            pl.BlockSpec((D_MODEL, bn), lambda j: (0, j)),
            pl.BlockSpec((1, bn), lambda j: (0, j)),
        ],
        out_specs=pl.BlockSpec((b, bn), lambda j: (0, j)),
        out_shape=jax.ShapeDtypeStruct((b, n), F32),
        compiler_params=pltpu.CompilerParams(dimension_semantics=("arbitrary",)),
        name="mod",
    )(c, w_mod, b_mod.reshape(1, n))


def _proj_kernel(x_ref, mod_ref, gain_ref, wnat_ref, wt_ref, bf_ref, pos_ref, e_ref,
                 ka_ref, kb_ref, ki_ref, t_ref, wi_ref, carry_ref, *, tm, q_scale):
    si = pl.program_id(1)

    @pl.when(si == 0)
    def _():
        carry_ref[...] = jnp.zeros_like(carry_ref)

    x = x_ref[0]
    ms = jnp.mean(x * x, axis=-1, keepdims=True)
    xn = (x * lax.rsqrt(ms + EPS)) * gain_ref[...]
    shift = mod_ref[0, 0:1, :]
    scale = mod_ref[0, 1:2, :]
    h = (xn * (1.0 + scale) + shift).astype(BF16)

    pn = _dot(h, wnat_ref[...])
    pt = _dot_nt(wt_ref[...], h)

    lane = lax.broadcasted_iota(I32, (tm, LANES), 1)
    row = lax.broadcasted_iota(I32, (tm, LANES), 0)
    fl = pn[:, NAT_F * LANES:(NAT_F + 1) * LANES] + bf_ref[...]
    logf = jnp.minimum(fl, 0.0) - jnp.log(1.0 + jnp.exp(-jnp.abs(fl)))
    cum = jnp.where(lane < N_HEADS, logf, 0.0)
    step = 1
    while step < tm:
        cum = cum + jnp.where(row >= step, pltpu.roll(cum, step, axis=0), 0.0)
        step *= 2
    cum = carry_ref[...] + cum
    carry_ref[...] = cum[tm - 1:tm, :]

    f_hi, f_mid, f_lo = _split3(cum * (-LOG2E))
    pieces = (f_hi.astype(F32) + pltpu.roll(f_mid.astype(F32), N_HEADS, axis=1)
              + pltpu.roll(f_lo.astype(F32), 2 * N_HEADS, axis=1))
    faug = _dot(pieces.astype(BF16), e_ref[...])
    pos = pos_ref[...].astype(F32)
    low = lane < HEAD_DIM
    for pair in range(HEAD_PAIRS):
        for grp, out_ref, aug in ((NAT_KA, ka_ref, None), (NAT_KB, kb_ref, pos)):
            both = pn[:, (grp + pair) * LANES:(grp + pair + 1) * LANES]
            for odd in range(2):
                hh = 2 * pair + odd
                k_h = pltpu.roll(both, HEAD_DIM, axis=1) if odd else both
                tail = faug[:, hh * LANES:(hh + 1) * LANES] if aug is None else aug
                out_ref[0, hh] = jnp.where(low, k_h, tail).astype(BF16)
    ki_ref[0] = pn[:, NAT_KI * LANES:(NAT_KI + 1) * LANES].astype(BF16)

    def put(group, val):
        for cc in range(tm // TILE):
            t_ref[0, cc, group * GROUP_W:(group + 1) * GROUP_W, :] = (
                val[:, cc * TILE:(cc + 1) * TILE].astype(BF16))

    def rows(group):
        return pt[group * GROUP_W:(group + 1) * GROUP_W, :]

    def silu(v):
        return v / (1.0 + jnp.exp(-v))

    put(G_QA, rows(G_QA) * q_scale)
    put(G_VA, rows(G_VA))
    put(G_GA, silu(rows(G_GA)))
    put(G_QB, rows(G_QB) * q_scale)
    put(G_VB, rows(G_VB))
    put(G_GB, silu(rows(G_GB)))
    put(G_QI, rows(G_QI))
    wi_ref[0] = pt[N_TGROUPS * GROUP_W:N_TGROUPS * GROUP_W + N_HEADS, :]


def _proj_call(x, mod3, gain, w_nat, w_t, bf_row, pos_aug, e_mats, *, tm):
    b, s, _ = x.shape
    ns = s // tm
    nt = s // TILE
    t_rows = N_TGROUPS * GROUP_W
    kern = functools.partial(
        _proj_kernel, tm=tm,
        q_scale=float(HEAD_DIM ** -0.5 * LOG2E))
    const = dict(pipeline_mode=pl.Buffered(1))
    return pl.pallas_call(
        kern,
        grid=(b, ns),
        in_specs=[
            pl.BlockSpec((1, tm, D_MODEL), lambda i, j: (i, j, 0)),
            pl.BlockSpec((1, 3, D_MODEL), lambda i, j: (i, 0, 0)),
            pl.BlockSpec((1, D_MODEL), lambda i, j: (0, 0)),
            pl.BlockSpec((D_MODEL, N_NAT), lambda i, j: (0, 0), **const),
            pl.BlockSpec((t_rows + WI_ROWS, D_MODEL), lambda i, j: (0, 0), **const),
            pl.BlockSpec((1, LANES), lambda i, j: (0, 0)),
            pl.BlockSpec((tm, LANES), lambda i, j: (j, 0)),
            pl.BlockSpec((LANES, N_HEADS * LANES), lambda i, j: (0, 0), **const),
        ],
        out_specs=[
            pl.BlockSpec((1, N_HEADS, tm, LANES), lambda i, j: (i, 0, j, 0)),
            pl.BlockSpec((1, N_HEADS, tm, LANES), lambda i, j: (i, 0, j, 0)),
            pl.BlockSpec((1, tm, LANES), lambda i, j: (i, j, 0)),
            pl.BlockSpec((1, tm // TILE, t_rows, TILE), lambda i, j: (i, j, 0, 0)),
            pl.BlockSpec((1, N_HEADS, tm), lambda i, j: (i, 0, j)),
        ],
        out_shape=[
            jax.ShapeDtypeStruct((b, N_HEADS, s, LANES), BF16),
            jax.ShapeDtypeStruct((b, N_HEADS, s, LANES), BF16),
            jax.ShapeDtypeStruct((b, s, LANES), BF16),
            jax.ShapeDtypeStruct((b, nt, t_rows, TILE), BF16),
            jax.ShapeDtypeStruct((b, N_HEADS, s), F32),
        ],
        scratch_shapes=[pltpu.VMEM((1, LANES), F32)],
        compiler_params=pltpu.CompilerParams(
            dimension_semantics=("arbitrary", "arbitrary"), vmem_limit_bytes=VMEM_LIMIT),
        name="proj",
    )(x, mod3, gain, w_nat, w_t, bf_row, pos_aug, e_mats)


def _tile_rows(ki):
    if isinstance(ki, int):
        return pl.ds(ki * TILE, TILE)
    return pl.ds(pl.multiple_of(ki * TILE, TILE), TILE)


def _attend(n_last, hb, scores, mask_last, v_tile):
    t = TILE
    ones_rows = jnp.ones((ACC_ROWS - HEAD_DIM, t), BF16)

    def pv(j, ki, alpha, acc, p):
        v_aug = jnp.concatenate([v_tile(j, ki), ones_rows], axis=0)
        upd = _dot(v_aug, p)
        return upd if acc is None else alpha * acc + upd

    def exp_tile(s, m, tile_max):
        m_new = jnp.maximum(m, tile_max)
        return m_new, jnp.exp2(m - m_new), jnp.exp2(s - m_new).astype(BF16)

    s = [scores(j, 0) for j in range(hb)]
    m = [jnp.full((1, t), NEG, F32) for _ in range(hb)]
    acc = [None] * hb
    for ki in range(n_last):
        for j in range(hb):
            m[j], alpha, p = exp_tile(s[j], m[j], jnp.max(s[j], axis=0, keepdims=True))
            s[j] = scores(j, ki + 1)
            acc[j] = pv(j, ki, alpha, acc[j], p)
    for j in range(hb):
        last = mask_last(s[j])
        m[j], alpha, p = exp_tile(last, m[j], jnp.max(last, axis=0, keepdims=True))
        acc[j] = pv(j, n_last, alpha, acc[j], p)
    return acc


def _attend_result(acc, gate):
    return ((acc[:HEAD_DIM] / acc[HEAD_DIM:HEAD_DIM + 1]) * gate.astype(F32)).astype(BF16)


def _fox_kernel(q_ref, k_ref, v_ref, g_ref, qaug_ref, o_ref, *, hb):
    qi = pl.program_id(2)
    t = TILE
    qz = [jnp.concatenate([q_ref[0, 0, j * HEAD_DIM:(j + 1) * HEAD_DIM, :], qaug_ref[...]], axis=0)
          for j in range(hb)]
    causal = lax.broadcasted_iota(I32, (t, t), 0) <= lax.broadcasted_iota(I32, (t, t), 1)

    def scores(j, ki):
        k_t = k_ref[0, j, _tile_rows(ki), :]
        return _dot(k_t, qz[j])

    def mask_last(s):
        return jnp.where(causal, s, NEG)

    def v_tile(j, ki):
        return v_ref[0, ki, j * HEAD_DIM:(j + 1) * HEAD_DIM, :]

    def run(n_last):
        acc = _attend(n_last, hb, scores, mask_last, v_tile)
        for j in range(hb):
            rows = slice(j * HEAD_DIM, (j + 1) * HEAD_DIM)
            o_ref[0, 0, rows, :] = _attend_result(acc[j], g_ref[0, 0, rows, :])

    for qv in range(v_ref.shape[1]):
        pl.when(qi == qv)(functools.partial(run, qv))


def _fox_call(ka, t_arr, qaug, *, hb):
    b, _, s, _ = ka.shape
    nt = s // TILE
    gpb = GROUP_W // (HEAD_DIM * hb)
    rb = HEAD_DIM * hb
    return pl.pallas_call(
        functools.partial(_fox_kernel, hb=hb),
        grid=(b, N_HEADS // hb, nt),
        in_specs=[
            pl.BlockSpec((1, 1, rb, TILE), lambda i, h, q: (i, q, G_QA * gpb + h, 0)),
            pl.BlockSpec((1, hb, s, LANES), lambda i, h, q: (i, h, 0, 0)),
            pl.BlockSpec((1, nt, rb, TILE), lambda i, h, q: (i, 0, G_VA * gpb + h, 0)),
            pl.BlockSpec((1, 1, rb, TILE), lambda i, h, q: (i, q, G_GA * gpb + h, 0)),
            pl.BlockSpec((HEAD_DIM, TILE), lambda i, h, q: (0, 0)),
        ],
        out_specs=pl.BlockSpec((1, 1, rb, TILE), lambda i, h, q: (i, q, h, 0)),
        out_shape=jax.ShapeDtypeStruct((b, nt, GROUP_W, TILE), BF16),
        compiler_params=pltpu.CompilerParams(
            dimension_semantics=("arbitrary", "arbitrary", "arbitrary"),
            vmem_limit_bytes=VMEM_LIMIT),
        name="fox",
    )(t_arr, ka, t_arr, t_arr, qaug)


def _dsa_kernel(qi_ref, wi_ref, ki_ref, q_ref, k_ref, v_ref, g_ref, qaug_ref, o_ref,
                sc_ref, hi_ref, lo_ref, bias_ref, thr_ref, lim_ref, cgt_ref, cge_ref,
                *, top_k, idx_bits, idx_scale, hb):
    qidx = pl.program_id(1)
    t = TILE
    nkv = qidx + 1
    srow = lax.broadcasted_iota(I32, (t, t), 0)
    tcol = lax.broadcasted_iota(I32, (t, t), 1)
    zpad = jnp.zeros((LANES - HEAD_DIM, t), BF16)

    qiz = [jnp.concatenate([qi_ref[0, 0, j * HEAD_DIM:(j + 1) * HEAD_DIM, :], zpad], axis=0)
           for j in range(N_HEADS)]
    wrow = [wi_ref[0, j:j + 1, :] for j in range(N_HEADS)]

    def index_scores(ki):
        k_t = ki_ref[0, _tile_rows(ki), :]
        sc = jnp.zeros((t, t), F32)
        for j in range(N_HEADS):
            sc = sc + wrow[j] * jnp.maximum(_dot(k_t, qiz[j]), 0.0)
        return sc * idx_scale

    def put_scores(ki, sc):
        sc_ref[ki] = sc
        bits = lax.bitcast_convert_type(sc, I32)
        key = bits ^ ((bits >> 31) & 0x7FFFFFFF)
        hi_ref[ki] = (key >> 16).astype(I16)
        lo_ref[ki] = ((key & 0xFFFF) - 32768).astype(I16)

    def index_tiles(n_last):
        for ki in range(n_last):
            put_scores(ki, index_scores(ki))
        put_scores(n_last, jnp.where(srow <= tcol, index_scores(n_last), -jnp.inf))

    for qv in range(v_ref.shape[1]):
        pl.when(qidx == qv)(functools.partial(index_tiles, qv))

    def count(pred):
        def body(ki, c):
            sidx = srow + ki * t
            ind = jnp.where(pred(sc_ref[ki], sidx), 1, 0).astype(I32)
            return c + jnp.sum(ind.reshape(t // 8, 8, t), axis=0)
        c8 = lax.fori_loop(0, nkv, body, jnp.zeros((8, t), I32))
        return jnp.sum(c8, axis=0, keepdims=True)

    def count_packed(ref, pred, one, zero):
        def body(ki, c):
            ind = jnp.where(pred(ref[ki]), one, zero)
            parts = [ind[r:r + 16] for r in range(0, t, 16)]
            while len(parts) > 1:
                parts = [x + y for x, y in zip(parts[0::2], parts[1::2])]
            return c + parts[0]
        c16 = lax.fori_loop(0, nkv, body, jnp.zeros((16, t), one.dtype))
        return jnp.sum(c16.astype(F32), axis=0, keepdims=True).astype(I32)

    def count16(ref, pred):
        return count_packed(ref, pred, jnp.ones((), I16), jnp.zeros((), I16))

    def count_rounded(cand):
        return count_packed(sc_ref, lambda v: v.astype(BF16) >= cand, jnp.ones((), BF16), jnp.zeros((), BF16))

    def key_to_f32(key):
        return lax.bitcast_convert_type(key ^ ((key >> 31) & 0x7FFFFFFF), F32)

    def kth_largest16(ref, kth):
        c0 = count16(ref, lambda v: v >= jnp.int16(0))
        val0 = jnp.where(c0 >= kth, 0, -32768).astype(I32)

        def bit_body(i, val):
            cand = val + lax.shift_left(jnp.int32(1), 14 - i)
            cand16 = cand.astype(I16)
            return jnp.where(count16(ref, lambda v: v >= cand16) >= kth, cand, val)

        return lax.fori_loop(0, 15, bit_body, val0)

    def store_threshold(thr):
        def body(ki, c):
            sc = sc_ref[ki]
            gt = jnp.where(sc > thr, 1, 0).astype(I32).reshape(t // 8, 8, t)
            ge = jnp.where(sc >= thr, 1, 0).astype(I32).reshape(t // 8, 8, t)
            return c[0] + jnp.sum(gt, axis=0), c[1] + jnp.sum(ge, axis=0)
        zero8 = jnp.zeros((8, t), I32)
        gt8, ge8 = lax.fori_loop(0, nkv, body, (zero8, zero8))
        c_gt = jnp.sum(gt8, axis=0, keepdims=True)
        c_ge = jnp.sum(ge8, axis=0, keepdims=True)
        thr_ref[...] = thr
        cgt_ref[...] = c_gt
        cge_ref[...] = c_ge
        ok = (c_gt < top_k) & ((c_ge >= top_k) | (thr == -jnp.inf))
        return jnp.max(jnp.where(ok, 0, 1))

    thr_ref[...] = jnp.full((1, t), -jnp.inf, F32)
    lim_ref[...] = jnp.zeros((1, t), I32)

    @pl.when(nkv * t > top_k)
    def _():
        hi_inf = -32641
        thr_hi = kth_largest16(hi_ref, top_k)
        thr_hi16 = thr_hi.astype(I16)

        def p2(ki, c):
            hi = hi_ref[ki]
            lo_ref[ki] = jnp.where(hi > thr_hi16, jnp.int16(32767),
                                   jnp.where(hi == thr_hi16, lo_ref[ki], jnp.int16(-32768)))
            return c

        lax.fori_loop(0, nkv, p2, 0)
        thr_lo = kth_largest16(lo_ref, top_k)
        fast = jnp.where(thr_hi > hi_inf,
                         key_to_f32(lax.shift_left(thr_hi, 16) | (thr_lo + 32768)), -jnp.inf)
        wrong = store_threshold(fast)

        @pl.when(wrong > 0)
        def _():
            def coarse_value(key16):
                return key_to_f32(lax.shift_left(key16, 16)).astype(BF16)

            c0 = count_rounded(coarse_value(jnp.zeros((1, t), I32)))
            key0 = jnp.where(c0 >= top_k, 0, -32768).astype(I32)

            def coarse_body(i, key16):
                cand = key16 + lax.shift_left(jnp.int32(1), 14 - i)
                return jnp.where(count_rounded(coarse_value(cand)) >= top_k, cand, key16)

            key16 = lax.fori_loop(0, 15, coarse_body, key0)
            base = lax.shift_left(key16, 16) - 65536

            def fine_body(i, off):
                cand = off + lax.shift_left(jnp.int32(1), 16 - i)
                thr_c = key_to_f32(base + cand)
                return jnp.where(count(lambda sc, sidx: sc >= thr_c) >= top_k, cand, off)

            off = lax.fori_loop(0, 17, fine_body, jnp.zeros((1, t), I32))
            store_threshold(jnp.where(key16 > -32768, key_to_f32(base + off), -jnp.inf))

        thr = thr_ref[...]
        active = thr > -jnp.inf
        need = top_k - cgt_ref[...]
        big = jnp.int32(2 ** idx_bits - 1)
        lim_ref[...] = jnp.where(active, big, 0)
        excess = jnp.max(jnp.where(active & (cge_ref[...] > top_k), 1, 0))

        @pl.when(excess > 0)
        def _():
            def lim_body(i, lim):
                cand = lim + lax.shift_left(jnp.int32(1), idx_bits - 1 - i)
                c = count(lambda sc, sidx: (sc == thr) & (sidx < cand))
                return jnp.where(c <= need, cand, lim)

            lim = lax.fori_loop(0, idx_bits, lim_body, jnp.zeros((1, t), I32))
            lim_ref[...] = jnp.where(active, lim, 0)

    thr = thr_ref[...]
    lim = lim_ref[...]

    def p3(ki, c):
        sc = sc_ref[ki]
        sidx = srow + ki * t
        sel = (sc > thr) | ((sc == thr) & (sidx < lim))
        bias_ref[ki] = jnp.where(sel, 0.0, NEG).astype(F32)
        return c

    lax.fori_loop(0, nkv, p3, 0)

    def attend_heads(n_last):
        for j0 in range(0, N_HEADS, hb):
            qz = [jnp.concatenate([q_ref[0, 0, j * HEAD_DIM:(j + 1) * HEAD_DIM, :], qaug_ref[j]],
                                  axis=0) for j in range(j0, j0 + hb)]

            def scores(n, ki, j0=j0, qz=qz):
                k_t = k_ref[0, j0 + n, _tile_rows(ki), :]
                return _dot(k_t, qz[n]) + bias_ref[ki]

            def v_tile(n, ki, j0=j0):
                return v_ref[0, ki, (j0 + n) * HEAD_DIM:(j0 + n + 1) * HEAD_DIM, :]

            acc = _attend(n_last, hb, scores, lambda s: s, v_tile)
            for n, j in enumerate(range(j0, j0 + hb)):
                rows = slice(j * HEAD_DIM, (j + 1) * HEAD_DIM)
                o_ref[0, 0, rows, :] = _attend_result(acc[n], g_ref[0, 0, rows, :])

    for qv in range(v_ref.shape[1]):
        pl.when(qidx == qv)(functools.partial(attend_heads, qv))


def _dsa_call(t_arr, wi, ki, kb, qaug, *, hb):
    b, _, s, _ = kb.shape
    nt = s // TILE
    top_k = min(TOPK_MAX, s // 4)
    idx_bits = int(np.ceil(np.log2(s))) + 1
    kern = functools.partial(_dsa_kernel, top_k=top_k, idx_bits=idx_bits, hb=hb,
                             idx_scale=float(HEAD_DIM ** -0.5 * N_HEADS ** -0.5))
    return pl.pallas_call(
        kern,
        grid=(b, nt),
        in_specs=[
            pl.BlockSpec((1, 1, GROUP_W, TILE), lambda i, q: (i, q, G_QI, 0)),
            pl.BlockSpec((1, N_HEADS, TILE), lambda i, q: (i, 0, q)),
            pl.BlockSpec((1, s, LANES), lambda i, q: (i, 0, 0)),
            pl.BlockSpec((1, 1, GROUP_W, TILE), lambda i, q: (i, q, G_QB, 0)),
            pl.BlockSpec((1, N_HEADS, s, LANES), lambda i, q: (i, 0, 0, 0)),
            pl.BlockSpec((1, nt, GROUP_W, TILE), lambda i, q: (i, 0, G_VB, 0)),
            pl.BlockSpec((1, 1, GROUP_W, TILE), lambda i, q: (i, q, G_GB, 0)),
            pl.BlockSpec((N_HEADS, HEAD_DIM, TILE), lambda i, q: (0, 0, 0)),
        ],
        out_specs=pl.BlockSpec((1, 1, GROUP_W, TILE), lambda i, q: (i, q, 0, 0)),
        out_shape=jax.ShapeDtypeStruct((b, nt, GROUP_W, TILE), BF16),
        scratch_shapes=[
            pltpu.VMEM((nt, TILE, TILE), F32),
            pltpu.VMEM((nt, TILE, TILE), I16),
            pltpu.VMEM((nt, TILE, TILE), I16),
            pltpu.VMEM((nt, TILE, TILE), F32),
            pltpu.VMEM((1, TILE), F32),
            pltpu.VMEM((1, TILE), I32),
            pltpu.VMEM((1, TILE), I32),
            pltpu.VMEM((1, TILE), I32),
        ],
        compiler_params=pltpu.CompilerParams(
            dimension_semantics=("arbitrary", "arbitrary"), vmem_limit_bytes=VMEM_LIMIT),
        name="dsa",
    )(t_arr, wi, ki, t_arr, kb, t_arr, t_arr, qaug)


def _out_kernel(x_ref, ya_ref, yb_ref, w_ref, gate_ref, gain_ref, o_ref, *, nsub):
    for cc in range(nsub):
        rows = slice(cc * TILE, (cc + 1) * TILE)
        y_t = jnp.concatenate([ya_ref[0, cc], yb_ref[0, cc]], axis=0)
        y = _dot_tn(y_t, w_ref[...])
        z = x_ref[0, rows, :] + gate_ref[0] * y
        ms = jnp.mean(z * z, axis=-1, keepdims=True)
        o_ref[0, rows, :] = (z * lax.rsqrt(ms + EPS)) * gain_ref[...]


def _out_call(x, ya, yb, w_out, gate3, final_gain, *, tm):
    b, s, _ = x.shape
    nsub = tm // TILE
    return pl.pallas_call(
        functools.partial(_out_kernel, nsub=nsub),
        grid=(b, s // tm),
        in_specs=[
            pl.BlockSpec((1, tm, D_MODEL), lambda i, q: (i, q, 0)),
            pl.BlockSpec((1, nsub, GROUP_W, TILE), lambda i, q: (i, q, 0, 0)),
            pl.BlockSpec((1, nsub, GROUP_W, TILE), lambda i, q: (i, q, 0, 0)),
            pl.BlockSpec((2 * GROUP_W, D_MODEL), lambda i, q: (0, 0)),
            pl.BlockSpec((1, 1, D_MODEL), lambda i, q: (i, 0, 0)),
            pl.BlockSpec((1, D_MODEL), lambda i, q: (0, 0)),
        ],
        out_specs=pl.BlockSpec((1, tm, D_MODEL), lambda i, q: (i, q, 0)),
        out_shape=jax.ShapeDtypeStruct((b, s, D_MODEL), F32),
        compiler_params=pltpu.CompilerParams(
            dimension_semantics=("arbitrary", "arbitrary"), vmem_limit_bytes=VMEM_LIMIT),
        name="outproj",
    )(x, ya, yb, w_out, gate3, final_gain)


def _layout_weights(w_in):
    widths = (GROUP_W, GROUP_W, GROUP_W, N_HEADS, GROUP_W, GROUP_W, GROUP_W, GROUP_W,
              N_HEADS * HEAD_DIM, HEAD_DIM, N_HEADS, GROUP_W)
    offs = np.concatenate([[0], np.cumsum(widths)])
    (w_qa, w_ka, w_va, w_f, w_ga, w_qb, w_kb, w_vb, w_qi, w_ki, w_wi, w_gb) = (
        w_in[:, offs[n]:offs[n + 1]] for n in range(len(widths)))

    def pad_cols(w):
        return jnp.pad(w, ((0, 0), (0, LANES - w.shape[1])))

    w_nat = jnp.concatenate([w_ka, w_kb, pad_cols(w_f), pad_cols(w_ki)], axis=1)
    w_t = jnp.concatenate([w_qa, w_va, w_ga, w_qb, w_vb, w_gb, w_qi,
                           jnp.pad(w_wi, ((0, 0), (0, WI_ROWS - N_HEADS)))], axis=1).T
    return w_nat.astype(BF16), w_t.astype(BF16)


def _constants(s):
    e = np.zeros((LANES, N_HEADS * LANES), np.float32)
    for i in range(3):
        for h in range(N_HEADS):
            e[N_HEADS * i + h, h * LANES + HEAD_DIM + i] = 1.0
    pos = np.zeros((s, LANES), np.float32)
    sidx = np.arange(s)
    for i in range(3):
        pos[:, HEAD_DIM + i] = sidx // 64
        pos[:, HEAD_DIM + 3 + i] = sidx % 64
    qa = np.zeros((HEAD_DIM, TILE), np.float32)
    qa[0:3, :] = 1.0
    return jnp.asarray(e, BF16), jnp.asarray(pos, BF16), jnp.asarray(qa, BF16)


def _alibi_rows():
    i = jnp.arange(1, N_HEADS + 1, dtype=F32)
    slopes = jnp.exp2(-8.0 * i / N_HEADS) * LOG2E
    c64 = _split3(slopes * 64.0)
    c1 = _split3(slopes)
    rows = jnp.stack(list(c64) + list(c1), axis=1)
    rows = jnp.pad(rows, ((0, 0), (0, HEAD_DIM - 6)))
    return jnp.broadcast_to(rows[:, :, None], (N_HEADS, HEAD_DIM, TILE)).astype(BF16)


def kernel(x, c, w_mod, b_mod, norm_gain, w_in, b_forget, w_out, final_gain):
    b, s, d = x.shape
    assert d == D_MODEL and s % TILE == 0
    assert w_mod.shape[0] == 1, "single-layer block"
    tm = ROW_BLOCK if s % ROW_BLOCK == 0 else TILE

    mod = _mod_call(c, w_mod[0], b_mod[0])
    mod3 = mod.reshape(b, 3, D_MODEL)
    w_nat, w_t = _layout_weights(w_in[0])
    e_mats, pos_aug, qaug_a = _constants(s)
    bf_row = jnp.pad(b_forget[0], (0, LANES - N_HEADS)).reshape(1, LANES)

    ka, kb, ki, t_arr, wi = _proj_call(
        x, mod3, norm_gain[0].reshape(1, D_MODEL), w_nat, w_t, bf_row, pos_aug, e_mats, tm=tm)

    ya = _fox_call(ka, t_arr, qaug_a, hb=HEADS_PER_BODY)
    yb = _dsa_call(t_arr, wi, ki, kb, _alibi_rows(), hb=HEADS_PER_BODY)

    gate3 = mod3[:, 2:3, :]
    tm_out = OUT_ROW_BLOCK if s % OUT_ROW_BLOCK == 0 else tm
    return _out_call(x, ya, yb, w_out[0].astype(BF16), gate3, final_gain.reshape(1, D_MODEL), tm=tm_out)
```
